```python
import math
import jax, jax.numpy as jnp
from jax import lax
import numpy as np

D_MODEL = 1024
BATCH = 8
SEQ = 2048
DEPTH = 1
DEC_BATCH = 32
DEC_SEQ = 4
PAST_LEN = 16384
PAGE_SIZE = 128

N_ATT_HEADS = 4
QK_DIM = 64
V_DIM = 2 * QK_DIM
ATT_WIDTH = N_ATT_HEADS * V_DIM
GMLP_WIDTH = D_MODEL - ATT_WIDTH
N_GMLP_GROUPS = 4
GMLP_GROUP_CH = GMLP_WIDTH // N_GMLP_GROUPS
GMLP_CHUNK = 128
MIX_WIDTH = ATT_WIDTH + GMLP_WIDTH
Q_COLS = N_ATT_HEADS * 2 * QK_DIM
IN_COLS = 2 * Q_COLS + ATT_WIDTH + 2 * GMLP_WIDTH
D_FF = 4 * D_MODEL
Q_BLOCK = 128
EPS = 1e-6

kernel_name = "hymba_diffattn_chunkgmlp_decode_step"


def rms_norm(x, g):
    xf = x.astype(jnp.float32)
    y = xf * lax.rsqrt(jnp.mean(xf * xf, axis=-1, keepdims=True) + EPS)
    return (y * g.astype(jnp.float32)).astype(x.dtype)


def split_proj(hn, w_in, q_gain, k_gain, gv_gain):
    B, T, _ = hn.shape
    z = hn @ w_in
    i1 = Q_COLS
    i2 = 2 * Q_COLS
    i3 = i2 + ATT_WIDTH
    i4 = i3 + GMLP_WIDTH
    q = rms_norm(z[..., :i1].reshape(B, T, N_ATT_HEADS, 2, QK_DIM), q_gain)
    k = rms_norm(z[..., i1:i2].reshape(B, T, N_ATT_HEADS, 2, QK_DIM), k_gain)
    v = z[..., i2:i3].reshape(B, T, N_ATT_HEADS, V_DIM)
    u = jax.nn.gelu(z[..., i3:i4])
    gv = rms_norm(jax.nn.gelu(z[..., i4:]).reshape(B, T, N_GMLP_GROUPS, GMLP_GROUP_CH), gv_gain)
    return q, k, v, u, gv


def prompt_diff_attn(q, k, v, lam):
    B, S = q.shape[0], q.shape[1]
    n_blk = S // Q_BLOCK
    kpos = jnp.arange(S)
    scale = QK_DIM ** -0.5

    def block(i):
        q_blk = lax.dynamic_slice_in_dim(q, i * Q_BLOCK, Q_BLOCK, axis=1)
        s = jnp.einsum('bthcd,bshcd->bhcts', q_blk, k,
                       preferred_element_type=jnp.float32) * scale
        qpos = i * Q_BLOCK + jnp.arange(Q_BLOCK)
        mask = kpos[None, :] <= qpos[:, None]
        p = jax.nn.softmax(jnp.where(mask, s, -jnp.inf), axis=-1)
        a = p[:, :, 0] - lam * p[:, :, 1]
        return jnp.einsum('bhts,bshd->bthd', a.astype(v.dtype), v)

    out = lax.map(block, jnp.arange(n_blk))
    return out.transpose(1, 0, 2, 3, 4).reshape(B, S, N_ATT_HEADS, V_DIM)


def sample_diff_attn(q, k_new, v_new, k_past, v_past, lam):
    T = q.shape[1]
    P = k_past.shape[1]
    scale = QK_DIM ** -0.5
    s_past = jnp.einsum('bthcd,bshcd->bhcts', q, k_past,
                        preferred_element_type=jnp.float32) * scale
    s_new = jnp.einsum('bthcd,bshcd->bhcts', q, k_new,
                       preferred_element_type=jnp.float32) * scale
    causal = jnp.tril(jnp.ones((T, T), dtype=bool))
    s_new = jnp.where(causal, s_new, -jnp.inf)
    p = jax.nn.softmax(jnp.concatenate([s_past, s_new], axis=-1), axis=-1)
    a = (p[:, :, 0] - lam * p[:, :, 1]).astype(v_new.dtype)
    return (jnp.einsum('bhts,bshd->bthd', a[..., :P], v_past)
            + jnp.einsum('bhts,bshd->bthd', a[..., P:], v_new))


def chunk_gating(u, gv, w_s, b_s, chunk):
    B, T = u.shape[0], u.shape[1]
    n_c = T // chunk
    tri = jnp.tril(jnp.ones((chunk, chunk), dtype=bool))
    w = jnp.where(tri, w_s[:, :chunk, :chunk], 0)
    gv_c = gv.reshape(B, n_c, chunk, N_GMLP_GROUPS, GMLP_GROUP_CH)
    s = jnp.einsum('gts,bcsgd->bctgd', w, gv_c) + b_s[:, :chunk].T[None, None, :, :, None]
    return u * s.reshape(B, T, GMLP_WIDTH)


def merge_out(att, mix, subln_gain, lam_init, w_out):
    B, T = att.shape[0], att.shape[1]
    o_att = (rms_norm(att, subln_gain) * (1.0 - lam_init)).reshape(B, T, ATT_WIDTH)
    return jnp.concatenate([o_att, mix], axis=-1) @ w_out


def sq_relu_ffn(h, g, w_up, w_down):
    return jnp.square(jax.nn.relu(rms_norm(h, g) @ w_up)) @ w_down


def setup_inputs(seed: int = 0) -> dict:
    key = jax.random.key(seed)
    ks = jax.random.split(key, 24)
    f32 = jnp.float32
    n_pages = PAST_LEN // PAGE_SIZE
    n_used = DEC_BATCH * n_pages
    n_phys = n_used + max(1, n_used // 4)
    nrm = lambda k, shape, s: jax.random.normal(k, shape, f32) * s
    perm = jax.random.permutation(ks[4], n_phys)
    return {
        "x_prompt": nrm(ks[0], (BATCH, SEQ, D_MODEL), 1.0),
        "x_sample": nrm(ks[1], (DEC_BATCH, DEC_SEQ, D_MODEL), 1.0),
        "cache_k": nrm(ks[2], (DEPTH, n_phys, PAGE_SIZE, N_ATT_HEADS, 2, QK_DIM), 1.0),
        "cache_v": nrm(ks[3], (DEPTH, n_phys, PAGE_SIZE, N_ATT_HEADS, V_DIM), 1.0),
        "page_table": perm[:n_used].reshape(DEC_BATCH, n_pages).astype(jnp.int32),
        "norm_mix": 1.0 + nrm(ks[5], (DEPTH, D_MODEL), 0.02),
        "w_in": nrm(ks[6], (DEPTH, D_MODEL, IN_COLS), D_MODEL ** -0.5),
        "q_gain": 1.0 + nrm(ks[7], (DEPTH, QK_DIM), 0.02),
        "k_gain": 1.0 + nrm(ks[8], (DEPTH, QK_DIM), 0.02),
        "lambda_q1": nrm(ks[9], (DEPTH, QK_DIM), 0.1),
        "lambda_k1": nrm(ks[10], (DEPTH, QK_DIM), 0.1),
        "lambda_q2": nrm(ks[11], (DEPTH, QK_DIM), 0.1),
        "lambda_k2": nrm(ks[12], (DEPTH, QK_DIM), 0.1),
        "subln_gain": 1.0 + nrm(ks[13], (DEPTH, V_DIM), 0.02),
        "gv_gain": 1.0 + nrm(ks[14], (DEPTH, N_GMLP_GROUPS, GMLP_GROUP_CH), 0.02),
        "w_spatial": nrm(ks[15], (DEPTH, N_GMLP_GROUPS, GMLP_CHUNK, GMLP_CHUNK), GMLP_CHUNK ** -0.5),
        "b_spatial": 1.0 + nrm(ks[16], (DEPTH, N_GMLP_GROUPS, GMLP_CHUNK), 0.1),
        "w_out": nrm(ks[17], (DEPTH, MIX_WIDTH, D_MODEL), MIX_WIDTH ** -0.5),
        "norm_ffn": 1.0 + nrm(ks[18], (DEPTH, D_MODEL), 0.02),
        "w_up": nrm(ks[19], (DEPTH, D_MODEL, D_FF), D_MODEL ** -0.5),
        "w_down": nrm(ks[20], (DEPTH, D_FF, D_MODEL), D_FF ** -0.5),
    }


def reference(x_prompt, x_sample, cache_k, cache_v, page_table, norm_mix, w_in,
              q_gain, k_gain, lambda_q1, lambda_k1, lambda_q2, lambda_k2,
              subln_gain, gv_gain, w_spatial, b_spatial, w_out, norm_ffn,
              w_up, w_down):
    f32 = jnp.float32
    n_seq_d, n_pages = page_table.shape
    hp, hs = x_prompt, x_sample
    kp_l, vp_l, ks_l, vs_l, gs_l = [], [], [], [], []
    for l in range(DEPTH):
        lam_init = 0.8 - 0.6 * math.exp(-0.3 * l)
        lam = (jnp.exp(jnp.sum(lambda_q1[l].astype(f32) * lambda_k1[l].astype(f32)))
               - jnp.exp(jnp.sum(lambda_q2[l].astype(f32) * lambda_k2[l].astype(f32)))
               + lam_init)

        q, k, v, u, gv = split_proj(rms_norm(hp, norm_mix[l]), w_in[l],
                                    q_gain[l], k_gain[l], gv_gain[l])
        att = prompt_diff_attn(q, k, v, lam)
        mix = chunk_gating(u, gv, w_spatial[l], b_spatial[l], GMLP_CHUNK)
        hp = hp + merge_out(att, mix, subln_gain[l], lam_init, w_out[l])
        hp = hp + sq_relu_ffn(hp, norm_ffn[l], w_up[l], w_down[l])
        kp_l.append(k)
        vp_l.append(v)

        qs, kns, vns, us, gvs = split_proj(rms_norm(hs, norm_mix[l]), w_in[l],
                                           q_gain[l], k_gain[l], gv_gain[l])
        k_past = cache_k[l][page_table].reshape(
            n_seq_d, n_pages * PAGE_SIZE, N_ATT_HEADS, 2, QK_DIM)
        v_past = cache_v[l][page_table].reshape(
            n_seq_d, n_pages * PAGE_SIZE, N_ATT_HEADS, V_DIM)
        att_s = sample_diff_attn(qs, kns, vns, k_past, v_past, lam)
        mix_s = chunk_gating(us, gvs, w_spatial[l], b_spatial[l], hs.shape[1])
        hs = hs + merge_out(att_s, mix_s, subln_gain[l], lam_init, w_out[l])
        hs = hs + sq_relu_ffn(hs, norm_ffn[l], w_up[l], w_down[l])
        ks_l.append(kns)
        vs_l.append(vns)
        gs_l.append(gvs.reshape(hs.shape[0], hs.shape[1], GMLP_WIDTH))

    return (hp, hs, jnp.stack(kp_l), jnp.stack(vp_l), jnp.stack(ks_l),
            jnp.stack(vs_l), jnp.stack(gs_l))
```

```python
import functools
import math

import jax
import jax.numpy as jnp
from jax import lax
from jax.experimental import pallas as pl
from jax.experimental.pallas import tpu as pltpu

F32 = jnp.float32
BF16 = jnp.bfloat16

D_MODEL = 1024
N_HEADS = 4
QK_DIM = 64
V_DIM = 128
HEAD_COLS = 2 * QK_DIM
ATT_WIDTH = N_HEADS * V_DIM
GMLP_WIDTH = 512
N_GROUPS = 4
GROUP_CH = 128
GMLP_CHUNK = 128
D_FF = 4096
PAGE = 128
EPS = 1e-6
LANES = 128
VMEM_LIMIT = 56 * 1024 * 1024

GELU_C = math.sqrt(2.0 / math.pi)


def _gelu(x):
    return 0.5 * x * (1.0 + jnp.tanh(GELU_C * (x + 0.044715 * (x * x * x))))


def _const_spec(shape):
    nd = len(shape)
    return pl.BlockSpec(shape, lambda *_: (0,) * nd, pipeline_mode=pl.Buffered(1))


def _half_group_rms(t):
    lane = lax.broadcasted_iota(jnp.int32, (1, LANES), 1)
    lo = lane < QK_DIM
    outs = []
    for j in range(t.shape[1] // LANES):
        c = t[:, j * LANES:(j + 1) * LANES]
        c2 = c * c
        s_lo = jnp.sum(jnp.where(lo, c2, 0.0), axis=-1, keepdims=True)
        s_hi = jnp.sum(jnp.where(lo, 0.0, c2), axis=-1, keepdims=True)
        r = jnp.where(lo, lax.rsqrt(s_lo * (1.0 / QK_DIM) + EPS),
                      lax.rsqrt(s_hi * (1.0 / QK_DIM) + EPS))
        outs.append(c * r)
    return jnp.concatenate(outs, axis=-1)


def _proj_kernel(x_ref, nm_ref, w_ref, qg_ref, kg_ref, gvg_ref, ws_ref, bs_ref,
                 q_ref, kt_ref, ktb_ref, v_ref, vb_ref, mix_ref, gv_ref, *, chunk):
    tm = x_ref.shape[0]
    x = x_ref[...]
    ms = jnp.mean(x * x, axis=-1, keepdims=True)
    xn = (x * lax.rsqrt(ms + EPS) * nm_ref[...]).astype(BF16)

    def seg(i):
        return jnp.dot(xn, w_ref[:, i * 512:(i + 1) * 512], preferred_element_type=F32)

    lane = lax.broadcasted_iota(jnp.int32, (1, 512), 1)
    first_comp = (lane % HEAD_COLS) < QK_DIM

    qn = _half_group_rms(seg(0)) * (qg_ref[...] * (QK_DIM ** -0.5))
    q_ref[0] = jnp.where(first_comp, qn, 0.0).astype(BF16)
    q_ref[1] = jnp.where(first_comp, 0.0, qn).astype(BF16)

    kn = _half_group_rms(seg(1)) * kg_ref[...]
    knt = kn.T
    kt_ref[0] = knt
    ktb_ref[0] = knt.astype(BF16)

    v = seg(2)
    v_ref[...] = v
    vb_ref[...] = v.astype(BF16)

    u = _gelu(seg(3))
    gr = _gelu(seg(4))
    gvs = []
    for g in range(N_GROUPS):
        c = gr[:, g * GROUP_CH:(g + 1) * GROUP_CH]
        r = lax.rsqrt(jnp.mean(c * c, axis=-1, keepdims=True) + EPS)
        gvs.append(c * r * gvg_ref[:, g * GROUP_CH:(g + 1) * GROUP_CH])
    gv_ref[...] = jnp.concatenate(gvs, axis=-1)

    row = lax.broadcasted_iota(jnp.int32, (GMLP_CHUNK, GMLP_CHUNK), 0)
    col = lax.broadcasted_iota(jnp.int32, (GMLP_CHUNK, GMLP_CHUNK), 1)
    keep = (col <= row) & ((row // chunk) == (col // chunk))
    n_blk = tm // GMLP_CHUNK
    for g in range(N_GROUPS):
        wm = jnp.where(keep, ws_ref[g], 0.0).astype(BF16)
        rhs = jnp.concatenate(
            [gvs[g][b * GMLP_CHUNK:(b + 1) * GMLP_CHUNK].astype(BF16) for b in range(n_blk)],
            axis=-1)
        s = jnp.dot(wm, rhs, preferred_element_type=F32)
        bias = bs_ref[:, g:g + 1]
        for b in range(n_blk):
            sb = s[:, b * GROUP_CH:(b + 1) * GROUP_CH] + bias
            ub = u[b * GMLP_CHUNK:(b + 1) * GMLP_CHUNK, g * GROUP_CH:(g + 1) * GROUP_CH]
            mix_ref[b * GMLP_CHUNK:(b + 1) * GMLP_CHUNK,
                    g * GROUP_CH:(g + 1) * GROUP_CH] = (ub * sb).astype(BF16)


def _proj(x2d, seq, tm, chunk, nm, w_in_b, qg, kg, gvg, ws_t, bs_t):
    rows = x2d.shape[0]
    n_seq = rows // seq
    per_seq = seq // tm
    grid = (rows // tm,)
    row_spec = lambda w: pl.BlockSpec((tm, w), lambda i: (i, 0))
    out_shape = (
        jax.ShapeDtypeStruct((2, rows, 512), BF16),
        jax.ShapeDtypeStruct((n_seq, 512, seq), F32),
        jax.ShapeDtypeStruct((n_seq, 512, seq), BF16),
        jax.ShapeDtypeStruct((rows, 512), F32),
        jax.ShapeDtypeStruct((rows, 512), BF16),
        jax.ShapeDtypeStruct((rows, 512), BF16),
        jax.ShapeDtypeStruct((rows, 512), F32),
    )
    kt_spec = pl.BlockSpec((1, 512, tm), lambda i: (i // per_seq, 0, i % per_seq))
    return pl.pallas_call(
        functools.partial(_proj_kernel, chunk=chunk),
        grid=grid,
        in_specs=[
            row_spec(D_MODEL),
            _const_spec((1, D_MODEL)),
            _const_spec((D_MODEL, 2560)),
            _const_spec((1, 512)),
            _const_spec((1, 512)),
            _const_spec((1, 512)),
            _const_spec((N_GROUPS, GMLP_CHUNK, GMLP_CHUNK)),
            _const_spec((GMLP_CHUNK, N_GROUPS)),
        ],
        out_specs=(
            pl.BlockSpec((2, tm, 512), lambda i: (0, i, 0)),
            kt_spec, kt_spec,
            row_spec(512), row_spec(512), row_spec(512), row_spec(512),
        ),
        out_shape=out_shape,
        compiler_params=pltpu.CompilerParams(
            dimension_semantics=("arbitrary",), vmem_limit_bytes=VMEM_LIMIT),
        name="proj",
    )(x2d, nm, w_in_b, qg, kg, gvg, ws_t, bs_t)


def _lambda(lq1_ref, lk1_ref, lq2_ref, lk2_ref, lam_init):
    a = jnp.sum(lq1_ref[...] * lk1_ref[...], axis=-1, keepdims=True)
    b = jnp.sum(lq2_ref[...] * lk2_ref[...], axis=-1, keepdims=True)
    return jnp.exp(a) - jnp.exp(b) + lam_init


def _sub_ln(att, gain, lam_init):
    r = lax.rsqrt(jnp.mean(att * att, axis=-1, keepdims=True) + EPS)
    return att * r * gain * (1.0 - lam_init)


def _attn_kernel(q_ref, kt_ref, v_ref, lq1_ref, lk1_ref, lq2_ref, lk2_ref, sg_ref,
                 o_ref, m_sc, l_sc, acc_sc, *, tq, lam_init):
    qi = pl.program_id(2)
    q = q_ref[...].reshape(2 * tq, HEAD_COLS)

    m_sc[...] = jnp.full(m_sc.shape, -jnp.inf, F32)
    l_sc[...] = jnp.zeros(l_sc.shape, F32)
    acc_sc[...] = jnp.zeros(acc_sc.shape, F32)

    def step(j, masked):
        off = pl.multiple_of(j * tq, tq)
        s = jnp.dot(q, kt_ref[0, :, pl.ds(off, tq)], preferred_element_type=F32)
        if masked:
            row = lax.broadcasted_iota(jnp.int32, (2 * tq, tq), 0) % tq
            col = lax.broadcasted_iota(jnp.int32, (2 * tq, tq), 1)
            s = jnp.where(col <= row, s, -jnp.inf)
        m_old = m_sc[...]
        m_new = jnp.maximum(m_old, jnp.max(s, axis=-1, keepdims=True))
        p = jnp.exp(s - m_new)
        alpha = jnp.exp(m_old - m_new)
        l_sc[...] = alpha * l_sc[...] + jnp.sum(p, axis=-1, keepdims=True)
        acc_sc[...] = alpha * acc_sc[...] + jnp.dot(
            p.astype(BF16), v_ref[pl.ds(off, tq), :], preferred_element_type=F32)
        m_sc[...] = m_new

    def body(j, carry):
        step(j, False)
        return carry

    lax.fori_loop(0, qi, body, 0)
    step(qi, True)

    lam = _lambda(lq1_ref, lk1_ref, lq2_ref, lk2_ref, lam_init)
    o = acc_sc[...] / l_sc[...]
    att = o[:tq] - lam * o[tq:]
    o_ref[...] = _sub_ln(att, sg_ref[...], lam_init).astype(o_ref.dtype)


def _prompt_attn(q2, ktb, vb, lq1, lk1, lq2, lk2, sg, *, n_seq, seq, tq, lam_init):
    nq = seq // tq
    vec = lambda n: _const_spec((1, n))
    return pl.pallas_call(
        functools.partial(_attn_kernel, tq=tq, lam_init=lam_init),
        grid=(n_seq, N_HEADS, nq),
        in_specs=[
            pl.BlockSpec((2, tq, HEAD_COLS), lambda b, h, i: (0, b * nq + i, h)),
            pl.BlockSpec((1, HEAD_COLS, seq), lambda b, h, i: (b, h, 0)),
            pl.BlockSpec((seq, V_DIM), lambda b, h, i: (b, h)),
            vec(QK_DIM), vec(QK_DIM), vec(QK_DIM), vec(QK_DIM), vec(V_DIM),
        ],
        out_specs=pl.BlockSpec((tq, V_DIM), lambda b, h, i: (b * nq + i, h)),
        out_shape=jax.ShapeDtypeStruct((n_seq * seq, ATT_WIDTH), BF16),
        scratch_shapes=[
            pltpu.VMEM((2 * tq, 1), F32),
            pltpu.VMEM((2 * tq, 1), F32),
            pltpu.VMEM((2 * tq, V_DIM), F32),
        ],
        compiler_params=pltpu.CompilerParams(
            dimension_semantics=("arbitrary", "arbitrary", "arbitrary"),
            vmem_limit_bytes=VMEM_LIMIT),
        name="prompt_attn",
    )(q2, ktb, vb, lq1, lk1, lq2, lk2, sg)


Q_PAD = 8


def _decode_kernel(pt_ref, q_ref, ktn_ref, vn_ref, lq1_ref, lk1_ref, lq2_ref, lk2_ref,
                   sg_ref, *rest, n_pages, n_new, lam_init):
    k_refs = rest[:n_pages]
    v_refs = rest[n_pages:2 * n_pages]
    o_ref, qbd_sc, m_sc, l_sc, acc_sc = rest[2 * n_pages:]
    b = pl.program_id(0)
    j = pl.program_id(1)
    grp = 2 * Q_PAD
    rows = N_HEADS * grp

    @pl.when(j == 0)
    def _init():
        lane = lax.broadcasted_iota(jnp.int32, (Q_PAD, 512), 1)
        for h in range(N_HEADS):
            own = (lane // HEAD_COLS) == h
            for c in range(2):
                r0 = h * grp + c * Q_PAD
                qbd_sc[r0:r0 + Q_PAD, :] = jnp.where(own, q_ref[c, 0], 0.0)
        s = jnp.dot(qbd_sc[...].astype(BF16), ktn_ref[0], preferred_element_type=F32)
        t = lax.broadcasted_iota(jnp.int32, s.shape, 0) % Q_PAD
        col = lax.broadcasted_iota(jnp.int32, s.shape, 1)
        ok = ((col // n_new) == b) & ((col % n_new) <= t)
        s = jnp.where(ok, s, -jnp.inf)
        m = jnp.max(s, axis=-1, keepdims=True)
        p = jnp.exp(s - m)
        m_sc[...] = m
        l_sc[...] = jnp.sum(p, axis=-1, keepdims=True)
        pb = p.astype(BF16)
        for h in range(N_HEADS):
            acc_sc[h * grp:(h + 1) * grp, :] = jnp.dot(
                pb[h * grp:(h + 1) * grp], vn_ref[:, h * V_DIM:(h + 1) * V_DIM],
                preferred_element_type=F32)

    kt = jnp.concatenate(
        [r[0].astype(BF16) for r in k_refs], axis=-1)
    s = jnp.dot(qbd_sc[...].astype(BF16), kt, preferred_element_type=F32)
    m_old = m_sc[...]
    m_new = jnp.maximum(m_old, jnp.max(s, axis=-1, keepdims=True))
    p = jnp.exp(s - m_new)
    alpha = jnp.exp(m_old - m_new)
    l_sc[...] = alpha * l_sc[...] + jnp.sum(p, axis=-1, keepdims=True)
    m_sc[...] = m_new
    pb = p.astype(BF16)
    for h in range(N_HEADS):
        vh = jnp.concatenate(
            [r[0, pl.ds(h, PAGE, stride=N_HEADS), :].astype(BF16) for r in v_refs], axis=0)
        sl = slice(h * grp, (h + 1) * grp)
        acc_sc[sl, :] = alpha[sl] * acc_sc[sl, :] + jnp.dot(
            pb[sl], vh, preferred_element_type=F32)

    @pl.when(j == pl.num_programs(1) - 1)
    def _fin():
        lam = _lambda(lq1_ref, lk1_ref, lq2_ref, lk2_ref, lam_init)
        o = acc_sc[...] / l_sc[...]
        for h in range(N_HEADS):
            att = o[h * grp:h * grp + Q_PAD] - lam * o[h * grp + Q_PAD:(h + 1) * grp]
            o_ref[0, :, h * V_DIM:(h + 1) * V_DIM] = _sub_ln(
                att, sg_ref[...], lam_init).astype(o_ref.dtype)


def _decode_attn(page_table, qpad, ktn, vn, ckt, cv, lq1, lk1, lq2, lk2, sg,
                 *, n_pages, n_new, lam_init):
    n_seq, n_tab = page_table.shape
    steps = n_tab // n_pages
    rows = N_HEADS * 2 * Q_PAD
    vec = lambda n: pl.BlockSpec((1, n), lambda b, j, pt: (0, 0))

    def page_spec(shape, i):
        nz = (0,) * len(shape)
        return pl.BlockSpec((1,) + shape, lambda b, j, pt: (pt[b, j * n_pages + i],) + nz)

    k_specs = [page_spec((512, PAGE), i) for i in range(n_pages)]
    v_specs = [page_spec((PAGE * N_HEADS, V_DIM), i) for i in range(n_pages)]
    grid_spec = pltpu.PrefetchScalarGridSpec(
        num_scalar_prefetch=1,
        grid=(n_seq, steps),
        in_specs=[
            pl.BlockSpec((2, 1, Q_PAD, 512), lambda b, j, pt: (0, b, 0, 0)),
            pl.BlockSpec((1, 512, n_seq * n_new), lambda b, j, pt: (0, 0, 0)),
            pl.BlockSpec((n_seq * n_new, 512), lambda b, j, pt: (0, 0)),
            vec(QK_DIM), vec(QK_DIM), vec(QK_DIM), vec(QK_DIM), vec(V_DIM),
        ] + k_specs + v_specs,
        out_specs=pl.BlockSpec((1, Q_PAD, ATT_WIDTH), lambda b, j, pt: (b, 0, 0)),
        scratch_shapes=[
            pltpu.VMEM((rows, 512), F32),
            pltpu.VMEM((rows, 1), F32),
            pltpu.VMEM((rows, 1), F32),
            pltpu.VMEM((rows, V_DIM), F32),
        ],
    )
    return pl.pallas_call(
        functools.partial(_decode_kernel, n_pages=n_pages, n_new=n_new, lam_init=lam_init),
        grid_spec=grid_spec,
        out_shape=jax.ShapeDtypeStruct((n_seq, Q_PAD, ATT_WIDTH), F32),
        compiler_params=pltpu.CompilerParams(
            dimension_semantics=("arbitrary", "arbitrary"), vmem_limit_bytes=VMEM_LIMIT),
        name="decode_attn",
    )(page_table, qpad, ktn, vn, lq1, lk1, lq2, lk2, sg,
      *([ckt] * n_pages), *([cv] * n_pages))


FF_BLK = 1024


def _post_kernel(x_ref, att_ref, mix_ref, wo_ref, nf_ref, wu_ref, wd_ref, o_ref):
    o_ref[...] = (x_ref[...]
                  + jnp.dot(att_ref[...], wo_ref[:ATT_WIDTH, :], preferred_element_type=F32)
                  + jnp.dot(mix_ref[...], wo_ref[ATT_WIDTH:, :], preferred_element_type=F32))
    h = o_ref[...]
    ms = jnp.mean(h * h, axis=-1, keepdims=True)
    hn = (h * lax.rsqrt(ms + EPS) * nf_ref[...]).astype(BF16)
    out = h
    for c in range(D_FF // FF_BLK):
        up = jnp.dot(hn, wu_ref[:, c * FF_BLK:(c + 1) * FF_BLK], preferred_element_type=F32)
        a = jnp.square(jnp.maximum(up, 0.0)).astype(BF16)
        out = out + jnp.dot(a, wd_ref[c * FF_BLK:(c + 1) * FF_BLK, :], preferred_element_type=F32)
    o_ref[...] = out


def _post(x2d, att, mix, wo_b, nf, wu_b, wd_b, tm):
    rows = x2d.shape[0]
    row_spec = lambda w: pl.BlockSpec((tm, w), lambda i: (i, 0))
    return pl.pallas_call(
        _post_kernel,
        grid=(rows // tm,),
        in_specs=[
            row_spec(D_MODEL), row_spec(ATT_WIDTH), row_spec(GMLP_WIDTH),
            _const_spec((D_MODEL, D_MODEL)),
            _const_spec((1, D_MODEL)),
            _const_spec((D_MODEL, D_FF)),
            _const_spec((D_FF, D_MODEL)),
        ],
        out_specs=row_spec(D_MODEL),
        out_shape=jax.ShapeDtypeStruct((rows, D_MODEL), F32),
        compiler_params=pltpu.CompilerParams(
            dimension_semantics=("arbitrary",), vmem_limit_bytes=VMEM_LIMIT),
        name="post",
    )(x2d, att, mix, wo_b, nf, wu_b, wd_b)


def kernel(x_prompt, x_sample, cache_k, cache_v, page_table, norm_mix, w_in, q_gain, k_gain,
           lambda_q1, lambda_k1, lambda_q2, lambda_k2, subln_gain, gv_gain, w_spatial,
           b_spatial, w_out, norm_ffn, w_up, w_down):
    depth = w_in.shape[0]
    assert depth == 1, "single-layer step"
    n_seq, seq, _ = x_prompt.shape
    n_dec, n_new, _ = x_sample.shape
    l = 0
    lam_init = 0.8 - 0.6 * math.exp(-0.3 * l)

    w_in_b = w_in[l].astype(BF16)
    wo_b = w_out[l].astype(BF16)
    wu_b = w_up[l].astype(BF16)
    wd_b = w_down[l].astype(BF16)
    qg = jnp.tile(q_gain[l], 2 * N_HEADS)[None]
    kg = jnp.tile(k_gain[l], 2 * N_HEADS)[None]
    gvg = gv_gain[l].reshape(1, GMLP_WIDTH)
    nm = norm_mix[l][None]
    nf = norm_ffn[l][None]
    sg = subln_gain[l][None]
    lq1, lk1, lq2, lk2 = (a[l][None] for a in (lambda_q1, lambda_k1, lambda_q2, lambda_k2))

    def gating_params(chunk):
        reps = GMLP_CHUNK // chunk
        ws_t = jnp.tile(w_spatial[l][:, :chunk, :chunk], (1, reps, reps))
        bs_t = jnp.tile(b_spatial[l][:, :chunk], (1, reps)).T
        return ws_t, bs_t

    xp = x_prompt.reshape(n_seq * seq, D_MODEL)
    q2, kt, ktb, v, vb, mix, _ = _proj(
        xp, seq, 512, GMLP_CHUNK, nm, w_in_b, qg, kg, gvg, *gating_params(GMLP_CHUNK))
    att = _prompt_attn(q2, ktb, vb, lq1, lk1, lq2, lk2, sg,
                       n_seq=n_seq, seq=seq, tq=256, lam_init=lam_init)
    y_prompt = _post(xp, att, mix, wo_b, nf, wu_b, wd_b, 512).reshape(n_seq, seq, D_MODEL)
    k_prompt = kt.reshape(n_seq, N_HEADS, 2, QK_DIM, seq).transpose(0, 4, 1, 2, 3)[None]
    v_prompt = v.reshape(1, n_seq, seq, N_HEADS, V_DIM)

    rows_s = n_dec * n_new
    xs = x_sample.reshape(rows_s, D_MODEL)
    q2s, kts, ktbs, vs, vbs, mixs, gvs = _proj(
        xs, rows_s, rows_s, n_new, nm, w_in_b, qg, kg, gvg, *gating_params(n_new))
    qpad = jnp.pad(q2s.reshape(2, n_dec, n_new, 512).astype(F32),
                   ((0, 0), (0, 0), (0, Q_PAD - n_new), (0, 0)))
    n_phys = cache_k.shape[1]
    ckt = cache_k[l].transpose(0, 2, 3, 4, 1).reshape(n_phys, 512, PAGE)
    cv = cache_v[l].reshape(n_phys, PAGE * N_HEADS, V_DIM)
    att_s = _decode_attn(page_table, qpad, ktbs, vbs, ckt, cv, lq1, lk1, lq2, lk2, sg,
                         n_pages=8, n_new=n_new, lam_init=lam_init)
    att_s = att_s[:, :n_new].reshape(rows_s, ATT_WIDTH).astype(BF16)
    y_sample = _post(xs, att_s, mixs, wo_b, nf, wu_b, wd_b, rows_s).reshape(n_dec, n_new, D_MODEL)
    k_sample = kts[0].T.reshape(1, n_dec, n_new, N_HEADS, 2, QK_DIM)
    v_sample = vs.reshape(1, n_dec, n_new, N_HEADS, V_DIM)
    gv_sample = gvs.reshape(1, n_dec, n_new, GMLP_WIDTH)

    return (y_prompt, y_sample, k_prompt, v_prompt, k_sample, v_sample, gv_sample)
```

```python
import functools
import math

import jax
import jax.numpy as jnp
from jax import lax
from jax.experimental import pallas as pl
from jax.experimental.pallas import tpu as pltpu

F32 = jnp.float32
BF16 = jnp.bfloat16

D_MODEL = 1024
N_HEADS = 4
QK_DIM = 64
V_DIM = 128
HEAD_COLS = 2 * QK_DIM
ATT_WIDTH = N_HEADS * V_DIM
GMLP_WIDTH = 512
N_GROUPS = 4
GROUP_CH = 128
GMLP_CHUNK = 128
D_FF = 4096
PAGE = 128
EPS = 1e-6
LANES = 128
VMEM_LIMIT = 56 * 1024 * 1024

GELU_C = math.sqrt(2.0 / math.pi)


def _gelu(x):
    return 0.5 * x * (1.0 + jnp.tanh(GELU_C * (x + 0.044715 * (x * x * x))))


def _const_spec(shape):
    nd = len(shape)
    return pl.BlockSpec(shape, lambda *_: (0,) * nd, pipeline_mode=pl.Buffered(1))


def _half_group_rms(t):
    lane = lax.broadcasted_iota(jnp.int32, (1, LANES), 1)
    lo = lane < QK_DIM
    outs = []
    for j in range(t.shape[1] // LANES):
        c = t[:, j * LANES:(j + 1) * LANES]
        c2 = c * c
        s_lo = jnp.sum(jnp.where(lo, c2, 0.0), axis=-1, keepdims=True)
        s_hi = jnp.sum(jnp.where(lo, 0.0, c2), axis=-1, keepdims=True)
        r = jnp.where(lo, lax.rsqrt(s_lo * (1.0 / QK_DIM) + EPS),
                      lax.rsqrt(s_hi * (1.0 / QK_DIM) + EPS))
        outs.append(c * r)
    return jnp.concatenate(outs, axis=-1)


def _proj_kernel(x_ref, nm_ref, w_ref, qg_ref, kg_ref, gvg_ref, ws_ref, bs_ref,
                 q_ref, kt_ref, ktb_ref, v_ref, vb_ref, mix_ref, gv_ref, *, chunk):
    tm = x_ref.shape[0]
    x = x_ref[...]
    ms = jnp.mean(x * x, axis=-1, keepdims=True)
    xn = (x * lax.rsqrt(ms + EPS) * nm_ref[...]).astype(BF16)

    def seg(i):
        return jnp.dot(xn, w_ref[:, i * 512:(i + 1) * 512], preferred_element_type=F32)

    lane = lax.broadcasted_iota(jnp.int32, (1, 512), 1)
    first_comp = (lane % HEAD_COLS) < QK_DIM

    qn = _half_group_rms(seg(0)) * (qg_ref[...] * (QK_DIM ** -0.5))
    q_ref[0] = jnp.where(first_comp, qn, 0.0).astype(BF16)
    q_ref[1] = jnp.where(first_comp, 0.0, qn).astype(BF16)

    kn = _half_group_rms(seg(1)) * kg_ref[...]
    knt = kn.T
    kt_ref[0] = knt
    ktb_ref[0] = knt.astype(BF16)

    v = seg(2)
    v_ref[...] = v
    vb_ref[...] = v.astype(BF16)

    u = _gelu(seg(3))
    gr = _gelu(seg(4))
    gvs = []
    for g in range(N_GROUPS):
        c = gr[:, g * GROUP_CH:(g + 1) * GROUP_CH]
        r = lax.rsqrt(jnp.mean(c * c, axis=-1, keepdims=True) + EPS)
        gvs.append(c * r * gvg_ref[:, g * GROUP_CH:(g + 1) * GROUP_CH])
    gv_ref[...] = jnp.concatenate(gvs, axis=-1)

    row = lax.broadcasted_iota(jnp.int32, (GMLP_CHUNK, GMLP_CHUNK), 0)
    col = lax.broadcasted_iota(jnp.int32, (GMLP_CHUNK, GMLP_CHUNK), 1)
    keep = (col <= row) & ((row // chunk) == (col // chunk))
    n_blk = tm // GMLP_CHUNK
    for g in range(N_GROUPS):
        wm = jnp.where(keep, ws_ref[g], 0.0).astype(BF16)
        rhs = jnp.concatenate(
            [gvs[g][b * GMLP_CHUNK:(b + 1) * GMLP_CHUNK].astype(BF16) for b in range(n_blk)],
            axis=-1)
        s = jnp.dot(wm, rhs, preferred_element_type=F32)
        bias = bs_ref[:, g:g + 1]
        for b in range(n_blk):
            sb = s[:, b * GROUP_CH:(b + 1) * GROUP_CH] + bias
            ub = u[b * GMLP_CHUNK:(b + 1) * GMLP_CHUNK, g * GROUP_CH:(g + 1) * GROUP_CH]
            mix_ref[b * GMLP_CHUNK:(b + 1) * GMLP_CHUNK,
                    g * GROUP_CH:(g + 1) * GROUP_CH] = (ub * sb).astype(BF16)


def _proj(x2d, seq, tm, chunk, nm, w_in_b, qg, kg, gvg, ws_t, bs_t):
    rows = x2d.shape[0]
    n_seq = rows // seq
    per_seq = seq // tm
    grid = (rows // tm,)
    row_spec = lambda w: pl.BlockSpec((tm, w), lambda i: (i, 0))
    out_shape = (
        jax.ShapeDtypeStruct((2, rows, 512), BF16),
        jax.ShapeDtypeStruct((n_seq, 512, seq), F32),
        jax.ShapeDtypeStruct((n_seq, 512, seq), BF16),
        jax.ShapeDtypeStruct((rows, 512), F32),
        jax.ShapeDtypeStruct((rows, 512), BF16),
        jax.ShapeDtypeStruct((rows, 512), BF16),
        jax.ShapeDtypeStruct((rows, 512), F32),
    )
    kt_spec = pl.BlockSpec((1, 512, tm), lambda i: (i // per_seq, 0, i % per_seq))
    return pl.pallas_call(
        functools.partial(_proj_kernel, chunk=chunk),
        grid=grid,
        in_specs=[
            row_spec(D_MODEL),
            _const_spec((1, D_MODEL)),
            _const_spec((D_MODEL, 2560)),
            _const_spec((1, 512)),
            _const_spec((1, 512)),
            _const_spec((1, 512)),
            _const_spec((N_GROUPS, GMLP_CHUNK, GMLP_CHUNK)),
            _const_spec((GMLP_CHUNK, N_GROUPS)),
        ],
        out_specs=(
            pl.BlockSpec((2, tm, 512), lambda i: (0, i, 0)),
            kt_spec, kt_spec,
            row_spec(512), row_spec(512), row_spec(512), row_spec(512),
        ),
        out_shape=out_shape,
        compiler_params=pltpu.CompilerParams(
            dimension_semantics=("arbitrary",), vmem_limit_bytes=VMEM_LIMIT),
        name="proj",
    )(x2d, nm, w_in_b, qg, kg, gvg, ws_t, bs_t)


def _lambda(lq1_ref, lk1_ref, lq2_ref, lk2_ref, lam_init):
    a = jnp.sum(lq1_ref[...] * lk1_ref[...], axis=-1, keepdims=True)
    b = jnp.sum(lq2_ref[...] * lk2_ref[...], axis=-1, keepdims=True)
    return jnp.exp(a) - jnp.exp(b) + lam_init


def _sub_ln(att, gain, lam_init):
    r = lax.rsqrt(jnp.mean(att * att, axis=-1, keepdims=True) + EPS)
    return att * r * gain * (1.0 - lam_init)


def _attn_kernel(q_ref, kt_ref, v_ref, lq1_ref, lk1_ref, lq2_ref, lk2_ref, sg_ref,
                 o_ref, *, tq, nq, lam_init):
    qi = pl.program_id(2)

    for c in range(nq):
        @pl.when(qi == c)
        def _block(c=c):
            q = q_ref[...].reshape(2 * tq, HEAD_COLS)
            lo, hi = c * tq, (c + 1) * tq
            row = lax.broadcasted_iota(jnp.int32, (2 * tq, tq), 0) % tq
            col = lax.broadcasted_iota(jnp.int32, (2 * tq, tq), 1)
            s_d = jnp.dot(q, kt_ref[0, :, lo:hi], preferred_element_type=F32)
            s_d = jnp.where(col <= row, s_d, -jnp.inf)
            m = jnp.max(s_d, axis=-1, keepdims=True)
            if c:
                s_f = jnp.dot(q, kt_ref[0, :, :lo], preferred_element_type=F32)
                m = jnp.maximum(m, jnp.max(s_f, axis=-1, keepdims=True))
            p_d = jnp.exp(s_d - m)
            l = jnp.sum(p_d, axis=-1, keepdims=True)
            acc = jnp.dot(p_d.astype(BF16), v_ref[lo:hi, :], preferred_element_type=F32)
            if c:
                p_f = jnp.exp(s_f - m)
                l = l + jnp.sum(p_f, axis=-1, keepdims=True)
                acc = acc + jnp.dot(p_f.astype(BF16), v_ref[:lo, :], preferred_element_type=F32)
            lam = _lambda(lq1_ref, lk1_ref, lq2_ref, lk2_ref, lam_init)
            o = acc / l
            att = o[:tq] - lam * o[tq:]
            o_ref[...] = _sub_ln(att, sg_ref[...], lam_init).astype(o_ref.dtype)


def _prompt_attn(q2, ktb, vb, lq1, lk1, lq2, lk2, sg, *, n_seq, seq, tq, lam_init):
    nq = seq // tq
    vec = lambda n: _const_spec((1, n))
    return pl.pallas_call(
        functools.partial(_attn_kernel, tq=tq, nq=nq, lam_init=lam_init),
        grid=(n_seq, N_HEADS, nq),
        in_specs=[
            pl.BlockSpec((2, tq, HEAD_COLS), lambda b, h, i: (0, b * nq + i, h)),
            pl.BlockSpec((1, HEAD_COLS, seq), lambda b, h, i: (b, h, 0)),
            pl.BlockSpec((seq, V_DIM), lambda b, h, i: (b, h)),
            vec(QK_DIM), vec(QK_DIM), vec(QK_DIM), vec(QK_DIM), vec(V_DIM),
        ],
        out_specs=pl.BlockSpec((tq, V_DIM), lambda b, h, i: (b * nq + i, h)),
        out_shape=jax.ShapeDtypeStruct((n_seq * seq, ATT_WIDTH), BF16),
        compiler_params=pltpu.CompilerParams(
            dimension_semantics=("arbitrary", "arbitrary", "arbitrary"),
            vmem_limit_bytes=VMEM_LIMIT),
        name="prompt_attn",
    )(q2, ktb, vb, lq1, lk1, lq2, lk2, sg)


Q_PAD = 8


def _decode_kernel(pt_ref, q_ref, ktn_ref, vn_ref, lq1_ref, lk1_ref, lq2_ref, lk2_ref,
                   sg_ref, *rest, n_pages, n_new, lam_init):
    k_refs = rest[:n_pages]
    v_refs = rest[n_pages:2 * n_pages]
    o_ref, qbd_sc, m_sc, l_sc, acc_sc = rest[2 * n_pages:]
    b = pl.program_id(0)
    j = pl.program_id(1)
    grp = 2 * Q_PAD
    rows = N_HEADS * grp

    @pl.when(j == 0)
    def _init():
        lane = lax.broadcasted_iota(jnp.int32, (Q_PAD, 512), 1)
        for h in range(N_HEADS):
            own = (lane // HEAD_COLS) == h
            for c in range(2):
                r0 = h * grp + c * Q_PAD
                qbd_sc[r0:r0 + Q_PAD, :] = jnp.where(own, q_ref[c, 0], 0.0)
        s = jnp.dot(qbd_sc[...].astype(BF16), ktn_ref[0], preferred_element_type=F32)
        t = lax.broadcasted_iota(jnp.int32, s.shape, 0) % Q_PAD
        col = lax.broadcasted_iota(jnp.int32, s.shape, 1)
        ok = ((col // n_new) == b) & ((col % n_new) <= t)
        s = jnp.where(ok, s, -jnp.inf)
        m = jnp.max(s, axis=-1, keepdims=True)
        p = jnp.exp(s - m)
        m_sc[...] = m
        l_sc[...] = jnp.sum(p, axis=-1, keepdims=True)
        pb = p.astype(BF16)
        for h in range(N_HEADS):
            acc_sc[h * grp:(h + 1) * grp, :] = jnp.dot(
                pb[h * grp:(h + 1) * grp], vn_ref[:, h * V_DIM:(h + 1) * V_DIM],
                preferred_element_type=F32)

    kt = jnp.concatenate(
        [r[0].astype(BF16) for r in k_refs], axis=-1)
    s = jnp.dot(qbd_sc[...].astype(BF16), kt, preferred_element_type=F32)
    m_old = m_sc[...]
    m_new = jnp.maximum(m_old, jnp.max(s, axis=-1, keepdims=True))
    p = jnp.exp(s - m_new)
    alpha = jnp.exp(m_old - m_new)
    l_sc[...] = alpha * l_sc[...] + jnp.sum(p, axis=-1, keepdims=True)
    m_sc[...] = m_new
    pb = p.astype(BF16)
    for h in range(N_HEADS):
        vh = jnp.concatenate(
            [r[0, pl.ds(h, PAGE, stride=N_HEADS), :].astype(BF16) for r in v_refs], axis=0)
        sl = slice(h * grp, (h + 1) * grp)
        acc_sc[sl, :] = alpha[sl] * acc_sc[sl, :] + jnp.dot(
            pb[sl], vh, preferred_element_type=F32)

    @pl.when(j == pl.num_programs(1) - 1)
    def _fin():
        lam = _lambda(lq1_ref, lk1_ref, lq2_ref, lk2_ref, lam_init)
        o = acc_sc[...] / l_sc[...]
        for h in range(N_HEADS):
            att = o[h * grp:h * grp + Q_PAD] - lam * o[h * grp + Q_PAD:(h + 1) * grp]
            o_ref[0, :, h * V_DIM:(h + 1) * V_DIM] = _sub_ln(
                att, sg_ref[...], lam_init).astype(o_ref.dtype)


def _decode_attn(page_table, qpad, ktn, vn, ckt, cv, lq1, lk1, lq2, lk2, sg,
                 *, n_pages, n_new, lam_init):
    n_seq, n_tab = page_table.shape
    steps = n_tab // n_pages
    rows = N_HEADS * 2 * Q_PAD
    vec = lambda n: pl.BlockSpec((1, n), lambda b, j, pt: (0, 0))

    def page_spec(shape, i):
        nz = (0,) * len(shape)
        return pl.BlockSpec((1,) + shape, lambda b, j, pt: (pt[b, j * n_pages + i],) + nz)

    k_specs = [page_spec((512, PAGE), i) for i in range(n_pages)]
    v_specs = [page_spec((PAGE * N_HEADS, V_DIM), i) for i in range(n_pages)]
    grid_spec = pltpu.PrefetchScalarGridSpec(
        num_scalar_prefetch=1,
        grid=(n_seq, steps),
        in_specs=[
            pl.BlockSpec((2, 1, Q_PAD, 512), lambda b, j, pt: (0, b, 0, 0)),
            pl.BlockSpec((1, 512, n_seq * n_new), lambda b, j, pt: (0, 0, 0)),
            pl.BlockSpec((n_seq * n_new, 512), lambda b, j, pt: (0, 0)),
            vec(QK_DIM), vec(QK_DIM), vec(QK_DIM), vec(QK_DIM), vec(V_DIM),
        ] + k_specs + v_specs,
        out_specs=pl.BlockSpec((1, Q_PAD, ATT_WIDTH), lambda b, j, pt: (b, 0, 0)),
        scratch_shapes=[
            pltpu.VMEM((rows, 512), F32),
            pltpu.VMEM((rows, 1), F32),
            pltpu.VMEM((rows, 1), F32),
            pltpu.VMEM((rows, V_DIM), F32),
        ],
    )
    return pl.pallas_call(
        functools.partial(_decode_kernel, n_pages=n_pages, n_new=n_new, lam_init=lam_init),
        grid_spec=grid_spec,
        out_shape=jax.ShapeDtypeStruct((n_seq, Q_PAD, ATT_WIDTH), F32),
        compiler_params=pltpu.CompilerParams(
            dimension_semantics=("arbitrary", "arbitrary"), vmem_limit_bytes=VMEM_LIMIT),
        name="decode_attn",
    )(page_table, qpad, ktn, vn, lq1, lk1, lq2, lk2, sg,
      *([ckt] * n_pages), *([cv] * n_pages))


FF_BLK = 1024


def _post_kernel(x_ref, att_ref, mix_ref, wo_ref, nf_ref, wu_ref, wd_ref, o_ref):
    o_ref[...] = (x_ref[...]
                  + jnp.dot(att_ref[...], wo_ref[:ATT_WIDTH, :], preferred_element_type=F32)
                  + jnp.dot(mix_ref[...], wo_ref[ATT_WIDTH:, :], preferred_element_type=F32))
    h = o_ref[...]
    ms = jnp.mean(h * h, axis=-1, keepdims=True)
    hn = (h * lax.rsqrt(ms + EPS) * nf_ref[...]).astype(BF16)
    out = h
    for c in range(D_FF // FF_BLK):
        up = jnp.dot(hn, wu_ref[:, c * FF_BLK:(c + 1) * FF_BLK], preferred_element_type=F32)
        a = jnp.square(jnp.maximum(up, 0.0)).astype(BF16)
        out = out + jnp.dot(a, wd_ref[c * FF_BLK:(c + 1) * FF_BLK, :], preferred_element_type=F32)
    o_ref[...] = out


def _post(x2d, att, mix, wo_b, nf, wu_b, wd_b, tm):
    rows = x2d.shape[0]
    row_spec = lambda w: pl.BlockSpec((tm, w), lambda i: (i, 0))
    return pl.pallas_call(
        _post_kernel,
        grid=(rows // tm,),
        in_specs=[
            row_spec(D_MODEL), row_spec(ATT_WIDTH), row_spec(GMLP_WIDTH),
            _const_spec((D_MODEL, D_MODEL)),
            _const_spec((1, D_MODEL)),
            _const_spec((D_MODEL, D_FF)),
            _const_spec((D_FF, D_MODEL)),
        ],
        out_specs=row_spec(D_MODEL),
        out_shape=jax.ShapeDtypeStruct((rows, D_MODEL), F32),
        compiler_params=pltpu.CompilerParams(
            dimension_semantics=("arbitrary",), vmem_limit_bytes=VMEM_LIMIT),
        name="post",
    )(x2d, att, mix, wo_b, nf, wu_b, wd_b)


def kernel(x_prompt, x_sample, cache_k, cache_v, page_table, norm_mix, w_in, q_gain, k_gain,
           lambda_q1, lambda_k1, lambda_q2, lambda_k2, subln_gain, gv_gain, w_spatial,
           b_spatial, w_out, norm_ffn, w_up, w_down):
    depth = w_in.shape[0]
    assert depth == 1, "single-layer step"
    n_seq, seq, _ = x_prompt.shape
    n_dec, n_new, _ = x_sample.shape
    l = 0
    lam_init = 0.8 - 0.6 * math.exp(-0.3 * l)

    w_in_b = w_in[l].astype(BF16)
    wo_b = w_out[l].astype(BF16)
    wu_b = w_up[l].astype(BF16)
    wd_b = w_down[l].astype(BF16)
    qg = jnp.tile(q_gain[l], 2 * N_HEADS)[None]
    kg = jnp.tile(k_gain[l], 2 * N_HEADS)[None]
    gvg = gv_gain[l].reshape(1, GMLP_WIDTH)
    nm = norm_mix[l][None]
    nf = norm_ffn[l][None]
    sg = subln_gain[l][None]
    lq1, lk1, lq2, lk2 = (a[l][None] for a in (lambda_q1, lambda_k1, lambda_q2, lambda_k2))

    def gating_params(chunk):
        reps = GMLP_CHUNK // chunk
        ws_t = jnp.tile(w_spatial[l][:, :chunk, :chunk], (1, reps, reps))
        bs_t = jnp.tile(b_spatial[l][:, :chunk], (1, reps)).T
        return ws_t, bs_t

    xp = x_prompt.reshape(n_seq * seq, D_MODEL)
    q2, kt, ktb, v, vb, mix, _ = _proj(
        xp, seq, 512, GMLP_CHUNK, nm, w_in_b, qg, kg, gvg, *gating_params(GMLP_CHUNK))
    att = _prompt_attn(q2, ktb, vb, lq1, lk1, lq2, lk2, sg,
                       n_seq=n_seq, seq=seq, tq=256, lam_init=lam_init)
    y_prompt = _post(xp, att, mix, wo_b, nf, wu_b, wd_b, 512).reshape(n_seq, seq, D_MODEL)
    k_prompt = kt.reshape(n_seq, N_HEADS, 2, QK_DIM, seq).transpose(0, 4, 1, 2, 3)[None]
    v_prompt = v.reshape(1, n_seq, seq, N_HEADS, V_DIM)

    rows_s = n_dec * n_new
    xs = x_sample.reshape(rows_s, D_MODEL)
    q2s, kts, ktbs, vs, vbs, mixs, gvs = _proj(
        xs, rows_s, rows_s, n_new, nm, w_in_b, qg, kg, gvg, *gating_params(n_new))
    qpad = jnp.pad(q2s.reshape(2, n_dec, n_new, 512).astype(F32),
                   ((0, 0), (0, 0), (0, Q_PAD - n_new), (0, 0)))
    n_phys = cache_k.shape[1]
    ckt = cache_k[l].transpose(0, 2, 3, 4, 1).reshape(n_phys, 512, PAGE)
    cv = cache_v[l].reshape(n_phys, PAGE * N_HEADS, V_DIM)
    att_s = _decode_attn(page_table, qpad, ktbs, vbs, ckt, cv, lq1, lk1, lq2, lk2, sg,
                         n_pages=8, n_new=n_new, lam_init=lam_init)
    att_s = att_s[:, :n_new].reshape(rows_s, ATT_WIDTH).astype(BF16)
    y_sample = _post(xs, att_s, mixs, wo_b, nf, wu_b, wd_b, rows_s).reshape(n_dec, n_new, D_MODEL)
    k_sample = kts[0].T.reshape(1, n_dec, n_new, N_HEADS, 2, QK_DIM)
    v_sample = vs.reshape(1, n_dec, n_new, N_HEADS, V_DIM)
    gv_sample = gvs.reshape(1, n_dec, n_new, GMLP_WIDTH)

    return (y_prompt, y_sample, k_prompt, v_prompt, k_sample, v_sample, gv_sample)
```

```python
import functools
import math

import jax
import jax.numpy as jnp
from jax import lax
from jax.experimental import pallas as pl
from jax.experimental.pallas import tpu as pltpu

F32 = jnp.float32
BF16 = jnp.bfloat16

D_MODEL = 1024
N_HEADS = 4
QK_DIM = 64
V_DIM = 128
HEAD_COLS = 2 * QK_DIM
ATT_WIDTH = N_HEADS * V_DIM
GMLP_WIDTH = 512
N_GROUPS = 4
GROUP_CH = 128
GMLP_CHUNK = 128
D_FF = 4096
PAGE = 128
EPS = 1e-6
LANES = 128
VMEM_LIMIT = 56 * 1024 * 1024

GELU_C = math.sqrt(2.0 / math.pi)


def _gelu(x):
    return 0.5 * x * (1.0 + jnp.tanh(GELU_C * (x + 0.044715 * (x * x * x))))


def _const_spec(shape):
    nd = len(shape)
    return pl.BlockSpec(shape, lambda *_: (0,) * nd, pipeline_mode=pl.Buffered(1))


def _half_group_rms(t):
    lane = lax.broadcasted_iota(jnp.int32, (1, LANES), 1)
    lo = lane < QK_DIM
    outs = []
    for j in range(t.shape[1] // LANES):
        c = t[:, j * LANES:(j + 1) * LANES]
        c2 = c * c
        s_lo = jnp.sum(jnp.where(lo, c2, 0.0), axis=-1, keepdims=True)
        s_hi = jnp.sum(jnp.where(lo, 0.0, c2), axis=-1, keepdims=True)
        r = jnp.where(lo, lax.rsqrt(s_lo * (1.0 / QK_DIM) + EPS),
                      lax.rsqrt(s_hi * (1.0 / QK_DIM) + EPS))
        outs.append(c * r)
    return jnp.concatenate(outs, axis=-1)


def _proj_kernel(x_ref, nm_ref, w_ref, qg_ref, kg_ref, gvg_ref, ws_ref, bs_ref,
                 q_ref, kt_ref, ktb_ref, v_ref, vb_ref, mix_ref, gv_ref, *, chunk):
    tm = x_ref.shape[0]
    x = x_ref[...]
    ms = jnp.mean(x * x, axis=-1, keepdims=True)
    xn = (x * lax.rsqrt(ms + EPS) * nm_ref[...]).astype(BF16)

    def seg(i):
        return jnp.dot(xn, w_ref[:, i * 512:(i + 1) * 512], preferred_element_type=F32)

    lane = lax.broadcasted_iota(jnp.int32, (1, 512), 1)
    first_comp = (lane % HEAD_COLS) < QK_DIM

    qn = _half_group_rms(seg(0)) * (qg_ref[...] * (QK_DIM ** -0.5))
    q_ref[0] = jnp.where(first_comp, qn, 0.0).astype(BF16)
    q_ref[1] = jnp.where(first_comp, 0.0, qn).astype(BF16)

    kn = _half_group_rms(seg(1)) * kg_ref[...]
    knt = kn.T
    kt_ref[0] = knt
    ktb_ref[0] = knt.astype(BF16)

    v = seg(2)
    v_ref[...] = v
    vb_ref[...] = v.astype(BF16)

    u = _gelu(seg(3))
    gr = _gelu(seg(4))
    gvs = []
    for g in range(N_GROUPS):
        c = gr[:, g * GROUP_CH:(g + 1) * GROUP_CH]
        r = lax.rsqrt(jnp.mean(c * c, axis=-1, keepdims=True) + EPS)
        gvs.append(c * r * gvg_ref[:, g * GROUP_CH:(g + 1) * GROUP_CH])
    gv_ref[...] = jnp.concatenate(gvs, axis=-1)

    row = lax.broadcasted_iota(jnp.int32, (GMLP_CHUNK, GMLP_CHUNK), 0)
    col = lax.broadcasted_iota(jnp.int32, (GMLP_CHUNK, GMLP_CHUNK), 1)
    keep = (col <= row) & ((row // chunk) == (col // chunk))
    n_blk = tm // GMLP_CHUNK
    for g in range(N_GROUPS):
        wm = jnp.where(keep, ws_ref[g], 0.0).astype(BF16)
        rhs = jnp.concatenate(
            [gvs[g][b * GMLP_CHUNK:(b + 1) * GMLP_CHUNK].astype(BF16) for b in range(n_blk)],
            axis=-1)
        s = jnp.dot(wm, rhs, preferred_element_type=F32)
        bias = bs_ref[:, g:g + 1]
        for b in range(n_blk):
            sb = s[:, b * GROUP_CH:(b + 1) * GROUP_CH] + bias
            ub = u[b * GMLP_CHUNK:(b + 1) * GMLP_CHUNK, g * GROUP_CH:(g + 1) * GROUP_CH]
            mix_ref[b * GMLP_CHUNK:(b + 1) * GMLP_CHUNK,
                    g * GROUP_CH:(g + 1) * GROUP_CH] = (ub * sb).astype(BF16)


def _proj(x2d, seq, tm, chunk, nm, w_in_b, qg, kg, gvg, ws_t, bs_t):
    rows = x2d.shape[0]
    n_seq = rows // seq
    per_seq = seq // tm
    grid = (rows // tm,)
    row_spec = lambda w: pl.BlockSpec((tm, w), lambda i: (i, 0))
    out_shape = (
        jax.ShapeDtypeStruct((2, rows, 512), BF16),
        jax.ShapeDtypeStruct((n_seq, 512, seq), F32),
        jax.ShapeDtypeStruct((n_seq, 512, seq), BF16),
        jax.ShapeDtypeStruct((rows, 512), F32),
        jax.ShapeDtypeStruct((rows, 512), BF16),
        jax.ShapeDtypeStruct((rows, 512), BF16),
        jax.ShapeDtypeStruct((rows, 512), F32),
    )
    kt_spec = pl.BlockSpec((1, 512, tm), lambda i: (i // per_seq, 0, i % per_seq))
    return pl.pallas_call(
        functools.partial(_proj_kernel, chunk=chunk),
        grid=grid,
        in_specs=[
            row_spec(D_MODEL),
            _const_spec((1, D_MODEL)),
            _const_spec((D_MODEL, 2560)),
            _const_spec((1, 512)),
            _const_spec((1, 512)),
            _const_spec((1, 512)),
            _const_spec((N_GROUPS, GMLP_CHUNK, GMLP_CHUNK)),
            _const_spec((GMLP_CHUNK, N_GROUPS)),
        ],
        out_specs=(
            pl.BlockSpec((2, tm, 512), lambda i: (0, i, 0)),
            kt_spec, kt_spec,
            row_spec(512), row_spec(512), row_spec(512), row_spec(512),
        ),
        out_shape=out_shape,
        compiler_params=pltpu.CompilerParams(
            dimension_semantics=("arbitrary",), vmem_limit_bytes=VMEM_LIMIT),
        name="proj",
    )(x2d, nm, w_in_b, qg, kg, gvg, ws_t, bs_t)


def _lambda(lq1_ref, lk1_ref, lq2_ref, lk2_ref, lam_init):
    a = jnp.sum(lq1_ref[...] * lk1_ref[...], axis=-1, keepdims=True)
    b = jnp.sum(lq2_ref[...] * lk2_ref[...], axis=-1, keepdims=True)
    return jnp.exp(a) - jnp.exp(b) + lam_init


def _sub_ln(att, gain, lam_init):
    r = lax.rsqrt(jnp.mean(att * att, axis=-1, keepdims=True) + EPS)
    return att * r * gain * (1.0 - lam_init)


def _attn_kernel(q_ref, kt_ref, v_ref, lq1_ref, lk1_ref, lq2_ref, lk2_ref, sg_ref,
                 o_ref, *, tq, nq, lam_init):
    qi = pl.program_id(2)

    for c in range(nq):
        @pl.when(qi == c)
        def _block(c=c):
            q = q_ref[...].reshape(2 * tq, HEAD_COLS)
            lo, hi = c * tq, (c + 1) * tq
            row = lax.broadcasted_iota(jnp.int32, (2 * tq, tq), 0) % tq
            col = lax.broadcasted_iota(jnp.int32, (2 * tq, tq), 1)
            s_d = jnp.dot(q, kt_ref[0, :, lo:hi], preferred_element_type=F32)
            s_d = jnp.where(col <= row, s_d, -jnp.inf)
            m = jnp.max(s_d, axis=-1, keepdims=True)
            if c:
                s_f = jnp.dot(q, kt_ref[0, :, :lo], preferred_element_type=F32)
                m = jnp.maximum(m, jnp.max(s_f, axis=-1, keepdims=True))
            p_d = jnp.exp(s_d - m)
            l = jnp.sum(p_d, axis=-1, keepdims=True)
            acc = jnp.dot(p_d.astype(BF16), v_ref[lo:hi, :], preferred_element_type=F32)
            if c:
                p_f = jnp.exp(s_f - m)
                l = l + jnp.sum(p_f, axis=-1, keepdims=True)
                acc = acc + jnp.dot(p_f.astype(BF16), v_ref[:lo, :], preferred_element_type=F32)
            lam = _lambda(lq1_ref, lk1_ref, lq2_ref, lk2_ref, lam_init)
            o = acc / l
            att = o[:tq] - lam * o[tq:]
            o_ref[...] = _sub_ln(att, sg_ref[...], lam_init).astype(o_ref.dtype)


def _prompt_attn(q2, ktb, vb, lq1, lk1, lq2, lk2, sg, *, n_seq, seq, tq, lam_init):
    nq = seq // tq
    vec = lambda n: _const_spec((1, n))
    return pl.pallas_call(
        functools.partial(_attn_kernel, tq=tq, nq=nq, lam_init=lam_init),
        grid=(n_seq, N_HEADS, nq),
        in_specs=[
            pl.BlockSpec((2, tq, HEAD_COLS), lambda b, h, i: (0, b * nq + i, h)),
            pl.BlockSpec((1, HEAD_COLS, seq), lambda b, h, i: (b, h, 0)),
            pl.BlockSpec((seq, V_DIM), lambda b, h, i: (b, h)),
            vec(QK_DIM), vec(QK_DIM), vec(QK_DIM), vec(QK_DIM), vec(V_DIM),
        ],
        out_specs=pl.BlockSpec((tq, V_DIM), lambda b, h, i: (b * nq + i, h)),
        out_shape=jax.ShapeDtypeStruct((n_seq * seq, ATT_WIDTH), BF16),
        compiler_params=pltpu.CompilerParams(
            dimension_semantics=("arbitrary", "arbitrary", "arbitrary"),
            vmem_limit_bytes=VMEM_LIMIT),
        name="prompt_attn",
    )(q2, ktb, vb, lq1, lk1, lq2, lk2, sg)


Q_PAD = 8


GRP = 2 * Q_PAD


def _decode_init(b, q_ref, ktn_ref, vn_ref, qbd_sc, m_sc, l_sc, acc_sc, *, n_new):
    grp = GRP
    lane = lax.broadcasted_iota(jnp.int32, (Q_PAD, 512), 1)
    for h in range(N_HEADS):
        own = (lane // HEAD_COLS) == h
        for c in range(2):
            r0 = h * grp + c * Q_PAD
            qbd_sc[r0:r0 + Q_PAD, :] = jnp.where(own, q_ref[c, 0], 0.0)
    s = jnp.dot(qbd_sc[...].astype(BF16), ktn_ref[0], preferred_element_type=F32)
    t = lax.broadcasted_iota(jnp.int32, s.shape, 0) % Q_PAD
    col = lax.broadcasted_iota(jnp.int32, s.shape, 1)
    ok = ((col // n_new) == b) & ((col % n_new) <= t)
    s = jnp.where(ok, s, -jnp.inf)
    m = jnp.max(s, axis=-1, keepdims=True)
    p = jnp.exp(s - m)
    m_sc[...] = m
    l_sc[...] = jnp.sum(p, axis=-1, keepdims=True)
    pb = p.astype(BF16)
    for h in range(N_HEADS):
        acc_sc[h * grp:(h + 1) * grp, :] = jnp.dot(
            pb[h * grp:(h + 1) * grp], vn_ref[:, h * V_DIM:(h + 1) * V_DIM],
            preferred_element_type=F32)


def _decode_pages(k_refs, v_refs, qbd_sc, m_sc, l_sc, acc_sc):
    grp = GRP
    kt = jnp.concatenate(
        [r[0].astype(BF16) for r in k_refs], axis=-1)
    s = jnp.dot(qbd_sc[...].astype(BF16), kt, preferred_element_type=F32)
    m_old = m_sc[...]
    m_new = jnp.maximum(m_old, jnp.max(s, axis=-1, keepdims=True))
    p = jnp.exp(s - m_new)
    alpha = jnp.exp(m_old - m_new)
    l_sc[...] = alpha * l_sc[...] + jnp.sum(p, axis=-1, keepdims=True)
    m_sc[...] = m_new
    pb = p.astype(BF16)
    for hp in range(N_HEADS // 2):
        vh = jnp.concatenate(
            [jnp.concatenate(
                [r[0, pl.ds(2 * hp + e, PAGE, stride=N_HEADS), :].astype(BF16) for e in range(2)],
                axis=-1) for r in v_refs], axis=0)
        pv = jnp.dot(pb[2 * hp * grp:(2 * hp + 2) * grp], vh, preferred_element_type=F32)
        for e in range(2):
            sl = slice((2 * hp + e) * grp, (2 * hp + e + 1) * grp)
            acc_sc[sl, :] = alpha[sl] * acc_sc[sl, :] + pv[e * grp:(e + 1) * grp,
                                                           e * V_DIM:(e + 1) * V_DIM]


def _decode_finish(lam_refs, sg_ref, o_ref, l_sc, acc_sc, *, lam_init):
    grp = GRP
    lam = _lambda(*lam_refs, lam_init)
    o = acc_sc[...] / l_sc[...]
    for h in range(N_HEADS):
        att = o[h * grp:h * grp + Q_PAD] - lam * o[h * grp + Q_PAD:(h + 1) * grp]
        o_ref[0, :, h * V_DIM:(h + 1) * V_DIM] = _sub_ln(
            att, sg_ref[...], lam_init).astype(o_ref.dtype)


def _mix_out(x_ref, att_ref, mix_ref, wo_ref, nf_ref, o_ref):
    o_ref[...] = (x_ref[...]
                  + jnp.dot(att_ref[...], wo_ref[:ATT_WIDTH, :], preferred_element_type=F32)
                  + jnp.dot(mix_ref[...], wo_ref[ATT_WIDTH:, :], preferred_element_type=F32))
    h = o_ref[...]
    ms = jnp.mean(h * h, axis=-1, keepdims=True)
    return (h * lax.rsqrt(ms + EPS) * nf_ref[...]).astype(BF16)


def _ffn_cols(hn, wu, wd):
    up = jnp.dot(hn, wu, preferred_element_type=F32)
    return jnp.dot(jnp.square(jnp.maximum(up, 0.0)).astype(BF16), wd, preferred_element_type=F32)


FF_BLK = 1024


def _post_kernel(x_ref, att_ref, mix_ref, wo_ref, nf_ref, wu_ref, wd_ref, o_ref):
    hn = _mix_out(x_ref, att_ref, mix_ref, wo_ref, nf_ref, o_ref)
    out = o_ref[...]
    for c in range(D_FF // FF_BLK):
        cols = slice(c * FF_BLK, (c + 1) * FF_BLK)
        out = out + _ffn_cols(hn, wu_ref[:, cols], wd_ref[cols, :])
    o_ref[...] = out


def _post(x2d, att, mix, wo_b, nf, wu_b, wd_b, tm):
    rows = x2d.shape[0]
    row_spec = lambda w: pl.BlockSpec((tm, w), lambda i: (i, 0))
    return pl.pallas_call(
        _post_kernel,
        grid=(rows // tm,),
        in_specs=[
            row_spec(D_MODEL), row_spec(ATT_WIDTH), row_spec(GMLP_WIDTH),
            _const_spec((D_MODEL, D_MODEL)),
            _const_spec((1, D_MODEL)),
            _const_spec((D_MODEL, D_FF)),
            _const_spec((D_FF, D_MODEL)),
        ],
        out_specs=row_spec(D_MODEL),
        out_shape=jax.ShapeDtypeStruct((rows, D_MODEL), F32),
        compiler_params=pltpu.CompilerParams(
            dimension_semantics=("arbitrary",), vmem_limit_bytes=VMEM_LIMIT),
        name="post",
    )(x2d, att, mix, wo_b, nf, wu_b, wd_b)


def _post_decode_kernel(pt_ref, x_ref, att_ref, mix_ref, wo_ref, nf_ref, wu_ref, wd_ref,
                        q_ref, ktn_ref, vn_ref, lq1_ref, lk1_ref, lq2_ref, lk2_ref, sg_ref,
                        *rest, n_pages, n_new, ff_blk, lam_init):
    k_refs = rest[:n_pages]
    v_refs = rest[n_pages:2 * n_pages]
    o_ref, os_ref, hn_sc, qbd_sc, m_sc, l_sc, acc_sc = rest[2 * n_pages:]
    i = pl.program_id(0)
    j = pl.program_id(1)
    n_steps = pl.num_programs(1)

    @pl.when(j == 0)
    def _():
        hn_sc[...] = _mix_out(x_ref, att_ref, mix_ref, wo_ref, nf_ref, o_ref)
        _decode_init(i, q_ref, ktn_ref, vn_ref, qbd_sc, m_sc, l_sc, acc_sc, n_new=n_new)

    c0 = pl.multiple_of(j * ff_blk, ff_blk)
    o_ref[...] += _ffn_cols(hn_sc[...], wu_ref[:, pl.ds(c0, ff_blk)], wd_ref[pl.ds(c0, ff_blk), :])
    _decode_pages(k_refs, v_refs, qbd_sc, m_sc, l_sc, acc_sc)

    @pl.when(j == n_steps - 1)
    def _():
        _decode_finish((lq1_ref, lk1_ref, lq2_ref, lk2_ref), sg_ref, os_ref, l_sc, acc_sc,
                       lam_init=lam_init)


def _post_decode(page_table, x2d, att, mix, wo_b, nf, wu_b, wd_b,
                 qpad, ktn, vn, ckt, cv, lq1, lk1, lq2, lk2, sg, *, n_new, lam_init):
    rows = x2d.shape[0]
    n_dec, n_tab = page_table.shape
    tm = rows // n_dec
    steps = 8
    ff_blk = D_FF // steps
    n_pages = n_tab // steps
    q_rows = N_HEADS * 2 * Q_PAD
    const = lambda shape: pl.BlockSpec(shape, lambda i, j, pt: (0,) * len(shape),
                                       pipeline_mode=pl.Buffered(1))
    row_spec = lambda w: pl.BlockSpec((tm, w), lambda i, j, pt: (i, 0))

    def page_spec(shape, p):
        nz = (0,) * len(shape)
        return pl.BlockSpec((1,) + shape, lambda i, j, pt: (pt[i, j * n_pages + p],) + nz)

    k_specs = [page_spec((512, PAGE), p) for p in range(n_pages)]
    v_specs = [page_spec((PAGE * N_HEADS, V_DIM), p) for p in range(n_pages)]
    grid_spec = pltpu.PrefetchScalarGridSpec(
        num_scalar_prefetch=1,
        grid=(n_dec, steps),
        in_specs=[
            row_spec(D_MODEL), row_spec(ATT_WIDTH), row_spec(GMLP_WIDTH),
            const((D_MODEL, D_MODEL)), const((1, D_MODEL)),
            const((D_MODEL, D_FF)), const((D_FF, D_MODEL)),
            pl.BlockSpec((2, 1, Q_PAD, 512), lambda i, j, pt: (0, i, 0, 0)),
            const((1, 512, n_dec * n_new)), const((n_dec * n_new, 512)),
            const((1, QK_DIM)), const((1, QK_DIM)), const((1, QK_DIM)), const((1, QK_DIM)),
            const((1, V_DIM)),
        ] + k_specs + v_specs,
        out_specs=(row_spec(D_MODEL),
                   pl.BlockSpec((1, Q_PAD, ATT_WIDTH), lambda i, j, pt: (i, 0, 0))),
        scratch_shapes=[
            pltpu.VMEM((tm, D_MODEL), BF16),
            pltpu.VMEM((q_rows, 512), F32),
            pltpu.VMEM((q_rows, 1), F32),
            pltpu.VMEM((q_rows, 1), F32),
            pltpu.VMEM((q_rows, V_DIM), F32),
        ],
    )
    return pl.pallas_call(
        functools.partial(_post_decode_kernel, n_pages=n_pages, n_new=n_new, ff_blk=ff_blk,
                          lam_init=lam_init),
        grid_spec=grid_spec,
        out_shape=(jax.ShapeDtypeStruct((rows, D_MODEL), F32),
                   jax.ShapeDtypeStruct((n_dec, Q_PAD, ATT_WIDTH), F32)),
        compiler_params=pltpu.CompilerParams(
            dimension_semantics=("arbitrary", "arbitrary"), vmem_limit_bytes=VMEM_LIMIT),
        name="post_decode",
    )(page_table, x2d, att, mix, wo_b, nf, wu_b, wd_b, qpad, ktn, vn, lq1, lk1, lq2, lk2, sg,
      *([ckt] * n_pages), *([cv] * n_pages))


def kernel(x_prompt, x_sample, cache_k, cache_v, page_table, norm_mix, w_in, q_gain, k_gain,
           lambda_q1, lambda_k1, lambda_q2, lambda_k2, subln_gain, gv_gain, w_spatial,
           b_spatial, w_out, norm_ffn, w_up, w_down):
    depth = w_in.shape[0]
    assert depth == 1, "single-layer step"
    n_seq, seq, _ = x_prompt.shape
    n_dec, n_new, _ = x_sample.shape
    l = 0
    lam_init = 0.8 - 0.6 * math.exp(-0.3 * l)

    w_in_b = w_in[l].astype(BF16)
    wo_b = w_out[l].astype(BF16)
    wu_b = w_up[l].astype(BF16)
    wd_b = w_down[l].astype(BF16)
    qg = jnp.tile(q_gain[l], 2 * N_HEADS)[None]
    kg = jnp.tile(k_gain[l], 2 * N_HEADS)[None]
    gvg = gv_gain[l].reshape(1, GMLP_WIDTH)
    nm = norm_mix[l][None]
    nf = norm_ffn[l][None]
    sg = subln_gain[l][None]
    lq1, lk1, lq2, lk2 = (a[l][None] for a in (lambda_q1, lambda_k1, lambda_q2, lambda_k2))

    def gating_params(chunk):
        reps = GMLP_CHUNK // chunk
        ws_t = jnp.tile(w_spatial[l][:, :chunk, :chunk], (1, reps, reps))
        bs_t = jnp.tile(b_spatial[l][:, :chunk], (1, reps)).T
        return ws_t, bs_t

    xp = x_prompt.reshape(n_seq * seq, D_MODEL)
    q2, kt, ktb, v, vb, mix, _ = _proj(
        xp, seq, 512, GMLP_CHUNK, nm, w_in_b, qg, kg, gvg, *gating_params(GMLP_CHUNK))
    att = _prompt_attn(q2, ktb, vb, lq1, lk1, lq2, lk2, sg,
                       n_seq=n_seq, seq=seq, tq=256, lam_init=lam_init)
    k_prompt = kt.reshape(n_seq, N_HEADS, 2, QK_DIM, seq).transpose(0, 4, 1, 2, 3)[None]
    v_prompt = v.reshape(1, n_seq, seq, N_HEADS, V_DIM)

    rows_s = n_dec * n_new
    xs = x_sample.reshape(rows_s, D_MODEL)
    q2s, kts, ktbs, vs, vbs, mixs, gvs = _proj(
        xs, rows_s, rows_s, n_new, nm, w_in_b, qg, kg, gvg, *gating_params(n_new))
    qpad = jnp.pad(q2s.reshape(2, n_dec, n_new, 512).astype(F32),
                   ((0, 0), (0, 0), (0, Q_PAD - n_new), (0, 0)))
    n_phys = cache_k.shape[1]
    ckt = cache_k[l].transpose(0, 2, 3, 4, 1).reshape(n_phys, 512, PAGE)
    cv = cache_v[l].reshape(n_phys, PAGE * N_HEADS, V_DIM)
    y_prompt, att_s = _post_decode(page_table, xp, att, mix, wo_b, nf, wu_b, wd_b,
                                   qpad, ktbs, vbs, ckt, cv, lq1, lk1, lq2, lk2, sg,
                                   n_new=n_new, lam_init=lam_init)
    y_prompt = y_prompt.reshape(n_seq, seq, D_MODEL)
    att_s = att_s[:, :n_new].reshape(rows_s, ATT_WIDTH).astype(BF16)
    y_sample = _post(xs, att_s, mixs, wo_b, nf, wu_b, wd_b, rows_s).reshape(n_dec, n_new, D_MODEL)
    k_sample = kts[0].T.reshape(1, n_dec, n_new, N_HEADS, 2, QK_DIM)
    v_sample = vs.reshape(1, n_dec, n_new, N_HEADS, V_DIM)
    gv_sample = gvs.reshape(1, n_dec, n_new, GMLP_WIDTH)

    return (y_prompt, y_sample, k_prompt, v_prompt, k_sample, v_sample, gv_sample)
```

```python
import functools
import math

import jax
import jax.numpy as jnp
from jax import lax
from jax.experimental import pallas as pl
from jax.experimental.pallas import tpu as pltpu

F32 = jnp.float32
BF16 = jnp.bfloat16

D_MODEL = 1024
N_HEADS = 4
QK_DIM = 64
V_DIM = 128
HEAD_COLS = 2 * QK_DIM
ATT_WIDTH = N_HEADS * V_DIM
GMLP_WIDTH = 512
N_GROUPS = 4
GROUP_CH = 128
GMLP_CHUNK = 128
D_FF = 4096
PAGE = 128
EPS = 1e-6
LANES = 128
VMEM_LIMIT = 56 * 1024 * 1024

GELU_C = math.sqrt(2.0 / math.pi)


def _gelu(x):
    return 0.5 * x * (1.0 + jnp.tanh(GELU_C * (x + 0.044715 * (x * x * x))))


def _const_spec(shape):
    nd = len(shape)
    return pl.BlockSpec(shape, lambda *_: (0,) * nd, pipeline_mode=pl.Buffered(1))


def _half_group_rms(t):
    lane = lax.broadcasted_iota(jnp.int32, (1, LANES), 1)
    lo = lane < QK_DIM
    outs = []
    for j in range(t.shape[1] // LANES):
        c = t[:, j * LANES:(j + 1) * LANES]
        c2 = c * c
        s_lo = jnp.sum(jnp.where(lo, c2, 0.0), axis=-1, keepdims=True)
        s_hi = jnp.sum(jnp.where(lo, 0.0, c2), axis=-1, keepdims=True)
        r = jnp.where(lo, lax.rsqrt(s_lo * (1.0 / QK_DIM) + EPS),
                      lax.rsqrt(s_hi * (1.0 / QK_DIM) + EPS))
        outs.append(c * r)
    return jnp.concatenate(outs, axis=-1)


def _proj_kernel(x_ref, nm_ref, w_ref, qg_ref, kg_ref, gvg_ref, ws_ref, bs_ref,
                 q_ref, kt_ref, ktb_ref, v_ref, vb_ref, mix_ref, gv_ref, *, chunk):
    tm = x_ref.shape[0]
    x = x_ref[...]
    ms = jnp.mean(x * x, axis=-1, keepdims=True)
    xn = (x * lax.rsqrt(ms + EPS) * nm_ref[...]).astype(BF16)

    def seg(i):
        return jnp.dot(xn, w_ref[:, i * 512:(i + 1) * 512], preferred_element_type=F32)

    lane = lax.broadcasted_iota(jnp.int32, (1, 512), 1)
    first_comp = (lane % HEAD_COLS) < QK_DIM

    qn = _half_group_rms(seg(0)) * (qg_ref[...] * (QK_DIM ** -0.5))
    q_ref[0] = jnp.where(first_comp, qn, 0.0).astype(BF16)
    q_ref[1] = jnp.where(first_comp, 0.0, qn).astype(BF16)

    kn = _half_group_rms(seg(1)) * kg_ref[...]
    knt = kn.T
    kt_ref[0] = knt
    ktb_ref[0] = knt.astype(BF16)

    v = seg(2)
    v_ref[...] = v
    vb_ref[...] = v.astype(BF16)

    u = _gelu(seg(3))
    gr = _gelu(seg(4))
    gvs = []
    for g in range(N_GROUPS):
        c = gr[:, g * GROUP_CH:(g + 1) * GROUP_CH]
        r = lax.rsqrt(jnp.mean(c * c, axis=-1, keepdims=True) + EPS)
        gvs.append(c * r * gvg_ref[:, g * GROUP_CH:(g + 1) * GROUP_CH])
    gv_ref[...] = jnp.concatenate(gvs, axis=-1)

    row = lax.broadcasted_iota(jnp.int32, (GMLP_CHUNK, GMLP_CHUNK), 0)
    col = lax.broadcasted_iota(jnp.int32, (GMLP_CHUNK, GMLP_CHUNK), 1)
    keep = (col <= row) & ((row // chunk) == (col // chunk))
    n_blk = tm // GMLP_CHUNK
    for g in range(N_GROUPS):
        wm = jnp.where(keep, ws_ref[g], 0.0).astype(BF16)
        rhs = jnp.concatenate(
            [gvs[g][b * GMLP_CHUNK:(b + 1) * GMLP_CHUNK].astype(BF16) for b in range(n_blk)],
            axis=-1)
        s = jnp.dot(wm, rhs, preferred_element_type=F32)
        bias = bs_ref[:, g:g + 1]
        for b in range(n_blk):
            sb = s[:, b * GROUP_CH:(b + 1) * GROUP_CH] + bias
            ub = u[b * GMLP_CHUNK:(b + 1) * GMLP_CHUNK, g * GROUP_CH:(g + 1) * GROUP_CH]
            mix_ref[b * GMLP_CHUNK:(b + 1) * GMLP_CHUNK,
                    g * GROUP_CH:(g + 1) * GROUP_CH] = (ub * sb).astype(BF16)


def _proj(x2d, seq, tm, chunk, nm, w_in_b, qg, kg, gvg, ws_t, bs_t):
    rows = x2d.shape[0]
    n_seq = rows // seq
    per_seq = seq // tm
    grid = (rows // tm,)
    row_spec = lambda w: pl.BlockSpec((tm, w), lambda i: (i, 0))
    out_shape = (
        jax.ShapeDtypeStruct((2, rows, 512), BF16),
        jax.ShapeDtypeStruct((n_seq, 512, seq), F32),
        jax.ShapeDtypeStruct((n_seq, 512, seq), BF16),
        jax.ShapeDtypeStruct((rows, 512), F32),
        jax.ShapeDtypeStruct((rows, 512), BF16),
        jax.ShapeDtypeStruct((rows, 512), BF16),
        jax.ShapeDtypeStruct((rows, 512), F32),
    )
    kt_spec = pl.BlockSpec((1, 512, tm), lambda i: (i // per_seq, 0, i % per_seq))
    return pl.pallas_call(
        functools.partial(_proj_kernel, chunk=chunk),
        grid=grid,
        in_specs=[
            row_spec(D_MODEL),
            _const_spec((1, D_MODEL)),
            _const_spec((D_MODEL, 2560)),
            _const_spec((1, 512)),
            _const_spec((1, 512)),
            _const_spec((1, 512)),
            _const_spec((N_GROUPS, GMLP_CHUNK, GMLP_CHUNK)),
            _const_spec((GMLP_CHUNK, N_GROUPS)),
        ],
        out_specs=(
            pl.BlockSpec((2, tm, 512), lambda i: (0, i, 0)),
            kt_spec, kt_spec,
            row_spec(512), row_spec(512), row_spec(512), row_spec(512),
        ),
        out_shape=out_shape,
        compiler_params=pltpu.CompilerParams(
            dimension_semantics=("arbitrary",), vmem_limit_bytes=VMEM_LIMIT),
        name="proj",
    )(x2d, nm, w_in_b, qg, kg, gvg, ws_t, bs_t)


def _lambda(lq1_ref, lk1_ref, lq2_ref, lk2_ref, lam_init):
    a = jnp.sum(lq1_ref[...] * lk1_ref[...], axis=-1, keepdims=True)
    b = jnp.sum(lq2_ref[...] * lk2_ref[...], axis=-1, keepdims=True)
    return jnp.exp(a) - jnp.exp(b) + lam_init


def _sub_ln(att, gain, lam_init):
    r = lax.rsqrt(jnp.mean(att * att, axis=-1, keepdims=True) + EPS)
    return att * r * gain * (1.0 - lam_init)


def _attn_kernel(q_ref, kt_ref, v_ref, lq1_ref, lk1_ref, lq2_ref, lk2_ref, sg_ref,
                 o_ref, *, tq, nq, lam_init):
    qi = pl.program_id(2)

    for c in range(nq):
        @pl.when(qi == c)
        def _block(c=c):
            q = q_ref[...].reshape(2 * tq, HEAD_COLS)
            lo, hi = c * tq, (c + 1) * tq
            row = lax.broadcasted_iota(jnp.int32, (2 * tq, tq), 0) % tq
            col = lax.broadcasted_iota(jnp.int32, (2 * tq, tq), 1)
            s_d = jnp.dot(q, kt_ref[0, :, lo:hi], preferred_element_type=F32)
            s_d = jnp.where(col <= row, s_d, -jnp.inf)
            m = jnp.max(s_d, axis=-1, keepdims=True)
            if c:
                s_f = jnp.dot(q, kt_ref[0, :, :lo], preferred_element_type=F32)
                m = jnp.maximum(m, jnp.max(s_f, axis=-1, keepdims=True))
            p_d = jnp.exp(s_d - m)
            l = jnp.sum(p_d, axis=-1, keepdims=True)
            acc = jnp.dot(p_d.astype(BF16), v_ref[lo:hi, :], preferred_element_type=F32)
            if c:
                p_f = jnp.exp(s_f - m)
                l = l + jnp.sum(p_f, axis=-1, keepdims=True)
                acc = acc + jnp.dot(p_f.astype(BF16), v_ref[:lo, :], preferred_element_type=F32)
            lam = _lambda(lq1_ref, lk1_ref, lq2_ref, lk2_ref, lam_init)
            o = acc / l
            att = o[:tq] - lam * o[tq:]
            o_ref[...] = _sub_ln(att, sg_ref[...], lam_init).astype(o_ref.dtype)


def _prompt_attn(q2, ktb, vb, lq1, lk1, lq2, lk2, sg, *, n_seq, seq, tq, lam_init):
    nq = seq // tq
    vec = lambda n: _const_spec((1, n))
    return pl.pallas_call(
        functools.partial(_attn_kernel, tq=tq, nq=nq, lam_init=lam_init),
        grid=(n_seq, N_HEADS, nq),
        in_specs=[
            pl.BlockSpec((2, tq, HEAD_COLS), lambda b, h, i: (0, b * nq + i, h)),
            pl.BlockSpec((1, HEAD_COLS, seq), lambda b, h, i: (b, h, 0)),
            pl.BlockSpec((seq, V_DIM), lambda b, h, i: (b, h)),
            vec(QK_DIM), vec(QK_DIM), vec(QK_DIM), vec(QK_DIM), vec(V_DIM),
        ],
        out_specs=pl.BlockSpec((tq, V_DIM), lambda b, h, i: (b * nq + i, h)),
        out_shape=jax.ShapeDtypeStruct((n_seq * seq, ATT_WIDTH), BF16),
        compiler_params=pltpu.CompilerParams(
            dimension_semantics=("arbitrary", "arbitrary", "arbitrary"),
            vmem_limit_bytes=VMEM_LIMIT),
        name="prompt_attn",
    )(q2, ktb, vb, lq1, lk1, lq2, lk2, sg)


Q_PAD = 8


GRP = 2 * Q_PAD


def _decode_init(b, q_ref, ktn_ref, vn_ref, qbd_sc, m_sc, l_sc, acc_sc, *, n_new):
    grp = GRP
    lane = lax.broadcasted_iota(jnp.int32, (Q_PAD, 512), 1)
    for h in range(N_HEADS):
        own = (lane // HEAD_COLS) == h
        for c in range(2):
            r0 = h * grp + c * Q_PAD
            qbd_sc[r0:r0 + Q_PAD, :] = jnp.where(own, q_ref[c, 0], 0.0)
    s = jnp.dot(qbd_sc[...].astype(BF16), ktn_ref[0], preferred_element_type=F32)
    t = lax.broadcasted_iota(jnp.int32, s.shape, 0) % Q_PAD
    col = lax.broadcasted_iota(jnp.int32, s.shape, 1)
    ok = ((col // n_new) == b) & ((col % n_new) <= t)
    s = jnp.where(ok, s, -jnp.inf)
    m = jnp.max(s, axis=-1, keepdims=True)
    p = jnp.exp(s - m)
    m_sc[...] = m
    l_sc[...] = jnp.sum(p, axis=-1, keepdims=True)
    pb = p.astype(BF16)
    for h in range(N_HEADS):
        acc_sc[h * grp:(h + 1) * grp, :] = jnp.dot(
            pb[h * grp:(h + 1) * grp], vn_ref[:, h * V_DIM:(h + 1) * V_DIM],
            preferred_element_type=F32)


def _decode_pages(k_refs, v_refs, qbd_sc, m_sc, l_sc, acc_sc):
    grp = GRP
    kt = jnp.concatenate(
        [r[...].astype(BF16) for r in k_refs], axis=-1)
    s = jnp.dot(qbd_sc[...].astype(BF16), kt, preferred_element_type=F32)
    m_old = m_sc[...]
    m_new = jnp.maximum(m_old, jnp.max(s, axis=-1, keepdims=True))
    p = jnp.exp(s - m_new)
    alpha = jnp.exp(m_old - m_new)
    l_sc[...] = alpha * l_sc[...] + jnp.sum(p, axis=-1, keepdims=True)
    m_sc[...] = m_new
    pb = p.astype(BF16)
    for hp in range(N_HEADS // 2):
        vh = jnp.concatenate(
            [jnp.concatenate(
                [r[pl.ds(2 * hp + e, PAGE, stride=N_HEADS), :].astype(BF16) for e in range(2)],
                axis=-1) for r in v_refs], axis=0)
        pv = jnp.dot(pb[2 * hp * grp:(2 * hp + 2) * grp], vh, preferred_element_type=F32)
        for e in range(2):
            sl = slice((2 * hp + e) * grp, (2 * hp + e + 1) * grp)
            acc_sc[sl, :] = alpha[sl] * acc_sc[sl, :] + pv[e * grp:(e + 1) * grp,
                                                           e * V_DIM:(e + 1) * V_DIM]


def _decode_finish(lam_refs, sg_ref, o_ref, l_sc, acc_sc, *, lam_init):
    grp = GRP
    lam = _lambda(*lam_refs, lam_init)
    o = acc_sc[...] / l_sc[...]
    for h in range(N_HEADS):
        att = o[h * grp:h * grp + Q_PAD] - lam * o[h * grp + Q_PAD:(h + 1) * grp]
        o_ref[0, :, h * V_DIM:(h + 1) * V_DIM] = _sub_ln(
            att, sg_ref[...], lam_init).astype(o_ref.dtype)


def _mix_out(x_ref, att_ref, mix_ref, wo_ref, nf_ref, o_ref):
    o_ref[...] = (x_ref[...]
                  + jnp.dot(att_ref[...], wo_ref[:ATT_WIDTH, :], preferred_element_type=F32)
                  + jnp.dot(mix_ref[...], wo_ref[ATT_WIDTH:, :], preferred_element_type=F32))
    h = o_ref[...]
    ms = jnp.mean(h * h, axis=-1, keepdims=True)
    return (h * lax.rsqrt(ms + EPS) * nf_ref[...]).astype(BF16)


def _ffn_cols(hn, wu, wd):
    up = jnp.dot(hn, wu, preferred_element_type=F32)
    return jnp.dot(jnp.square(jnp.maximum(up, 0.0)).astype(BF16), wd, preferred_element_type=F32)


FF_BLK = 1024


def _post_kernel(x_ref, att_ref, mix_ref, wo_ref, nf_ref, wu_ref, wd_ref, o_ref):
    hn = _mix_out(x_ref, att_ref, mix_ref, wo_ref, nf_ref, o_ref)
    out = o_ref[...]
    for c in range(D_FF // FF_BLK):
        cols = slice(c * FF_BLK, (c + 1) * FF_BLK)
        out = out + _ffn_cols(hn, wu_ref[:, cols], wd_ref[cols, :])
    o_ref[...] = out


def _post(x2d, att, mix, wo_b, nf, wu_b, wd_b, tm):
    rows = x2d.shape[0]
    row_spec = lambda w: pl.BlockSpec((tm, w), lambda i: (i, 0))
    return pl.pallas_call(
        _post_kernel,
        grid=(rows // tm,),
        in_specs=[
            row_spec(D_MODEL), row_spec(ATT_WIDTH), row_spec(GMLP_WIDTH),
            _const_spec((D_MODEL, D_MODEL)),
            _const_spec((1, D_MODEL)),
            _const_spec((D_MODEL, D_FF)),
            _const_spec((D_FF, D_MODEL)),
        ],
        out_specs=row_spec(D_MODEL),
        out_shape=jax.ShapeDtypeStruct((rows, D_MODEL), F32),
        compiler_params=pltpu.CompilerParams(
            dimension_semantics=("arbitrary",), vmem_limit_bytes=VMEM_LIMIT),
        name="post",
    )(x2d, att, mix, wo_b, nf, wu_b, wd_b)


N_SLOTS = 2


def _post_decode_kernel(pt_ref, x_ref, att_ref, mix_ref, wo_ref, nf_ref, wu_ref, wd_ref,
                        q_ref, ktn_ref, vn_ref, lq1_ref, lk1_ref, lq2_ref, lk2_ref, sg_ref,
                        ck_hbm, cv_hbm, o_ref, os_ref,
                        hn_sc, qbd_sc, m_sc, l_sc, acc_sc, kbuf, vbuf, sem,
                        *, n_pages, n_new, ff_blk, lam_init):
    i = pl.program_id(0)
    j = pl.program_id(1)
    n_tiles = pl.num_programs(0)
    n_steps = pl.num_programs(1)
    t = i * n_steps + j
    slot = t % N_SLOTS

    def group_copies(seq, grp_idx, s):
        cps = []
        for p in range(n_pages):
            page = pt_ref[seq, grp_idx * n_pages + p]
            cps.append(pltpu.make_async_copy(ck_hbm.at[page], kbuf.at[s, p], sem.at[s, 0]))
            cps.append(pltpu.make_async_copy(cv_hbm.at[page], vbuf.at[s, p], sem.at[s, 1]))
        return cps

    @pl.when(t == 0)
    def _():
        for cp in group_copies(0, 0, 0):
            cp.start()

    @pl.when(j == 0)
    def _():
        hn_sc[...] = _mix_out(x_ref, att_ref, mix_ref, wo_ref, nf_ref, o_ref)
        _decode_init(i, q_ref, ktn_ref, vn_ref, qbd_sc, m_sc, l_sc, acc_sc, n_new=n_new)

    t_next = (t + 1) % (n_tiles * n_steps)
    next_slot = (t + 1) % N_SLOTS
    for cp in group_copies(i, j, slot):
        cp.wait()
    for cp in group_copies(t_next // n_steps, t_next % n_steps, next_slot):
        cp.start()

    c0 = pl.multiple_of(j * ff_blk, ff_blk)
    o_ref[...] += _ffn_cols(hn_sc[...], wu_ref[:, pl.ds(c0, ff_blk)], wd_ref[pl.ds(c0, ff_blk), :])
    _decode_pages([kbuf.at[slot, p] for p in range(n_pages)],
                  [vbuf.at[slot, p] for p in range(n_pages)], qbd_sc, m_sc, l_sc, acc_sc)

    @pl.when(j == n_steps - 1)
    def _():
        _decode_finish((lq1_ref, lk1_ref, lq2_ref, lk2_ref), sg_ref, os_ref, l_sc, acc_sc,
                       lam_init=lam_init)

    @pl.when(t == n_tiles * n_steps - 1)
    def _():
        for cp in group_copies(0, 0, next_slot):
            cp.wait()


def _post_decode(page_table, x2d, att, mix, wo_b, nf, wu_b, wd_b,
                 qpad, ktn, vn, ckt, cv, lq1, lk1, lq2, lk2, sg, *, n_new, lam_init):
    rows = x2d.shape[0]
    n_dec, n_tab = page_table.shape
    tm = rows // n_dec
    steps = 8
    ff_blk = D_FF // steps
    n_pages = n_tab // steps
    q_rows = N_HEADS * 2 * Q_PAD
    const = lambda shape: pl.BlockSpec(shape, lambda i, j, pt: (0,) * len(shape),
                                       pipeline_mode=pl.Buffered(1))
    row_spec = lambda w: pl.BlockSpec((tm, w), lambda i, j, pt: (i, 0))
    page_buf = pltpu.VMEM((N_SLOTS, n_pages, 512, PAGE), F32)
    grid_spec = pltpu.PrefetchScalarGridSpec(
        num_scalar_prefetch=1,
        grid=(n_dec, steps),
        in_specs=[
            row_spec(D_MODEL), row_spec(ATT_WIDTH), row_spec(GMLP_WIDTH),
            const((D_MODEL, D_MODEL)), const((1, D_MODEL)),
            const((D_MODEL, D_FF)), const((D_FF, D_MODEL)),
            pl.BlockSpec((2, 1, Q_PAD, 512), lambda i, j, pt: (0, i, 0, 0)),
            const((1, 512, n_dec * n_new)), const((n_dec * n_new, 512)),
            const((1, QK_DIM)), const((1, QK_DIM)), const((1, QK_DIM)), const((1, QK_DIM)),
            const((1, V_DIM)),
            pl.BlockSpec(memory_space=pl.ANY),
            pl.BlockSpec(memory_space=pl.ANY),
        ],
        out_specs=(row_spec(D_MODEL),
                   pl.BlockSpec((1, Q_PAD, ATT_WIDTH), lambda i, j, pt: (i, 0, 0))),
        scratch_shapes=[
            pltpu.VMEM((tm, D_MODEL), BF16),
            pltpu.VMEM((q_rows, 512), F32),
            pltpu.VMEM((q_rows, 1), F32),
            pltpu.VMEM((q_rows, 1), F32),
            pltpu.VMEM((q_rows, V_DIM), F32),
            page_buf, page_buf,
            pltpu.SemaphoreType.DMA((N_SLOTS, 2)),
        ],
    )
    return pl.pallas_call(
        functools.partial(_post_decode_kernel, n_pages=n_pages, n_new=n_new, ff_blk=ff_blk,
                          lam_init=lam_init),
        grid_spec=grid_spec,
        out_shape=(jax.ShapeDtypeStruct((rows, D_MODEL), F32),
                   jax.ShapeDtypeStruct((n_dec, Q_PAD, ATT_WIDTH), F32)),
        compiler_params=pltpu.CompilerParams(
            dimension_semantics=("arbitrary", "arbitrary"), vmem_limit_bytes=VMEM_LIMIT),
        name="post_decode",
    )(page_table, x2d, att, mix, wo_b, nf, wu_b, wd_b, qpad, ktn, vn, lq1, lk1, lq2, lk2, sg,
      ckt, cv)


def kernel(x_prompt, x_sample, cache_k, cache_v, page_table, norm_mix, w_in, q_gain, k_gain,
           lambda_q1, lambda_k1, lambda_q2, lambda_k2, subln_gain, gv_gain, w_spatial,
           b_spatial, w_out, norm_ffn, w_up, w_down):
    depth = w_in.shape[0]
    assert depth == 1, "single-layer step"
    n_seq, seq, _ = x_prompt.shape
    n_dec, n_new, _ = x_sample.shape
    l = 0
    lam_init = 0.8 - 0.6 * math.exp(-0.3 * l)

    w_in_b = w_in[l].astype(BF16)
    wo_b = w_out[l].astype(BF16)
    wu_b = w_up[l].astype(BF16)
    wd_b = w_down[l].astype(BF16)
    qg = jnp.tile(q_gain[l], 2 * N_HEADS)[None]
    kg = jnp.tile(k_gain[l], 2 * N_HEADS)[None]
    gvg = gv_gain[l].reshape(1, GMLP_WIDTH)
    nm = norm_mix[l][None]
    nf = norm_ffn[l][None]
    sg = subln_gain[l][None]
    lq1, lk1, lq2, lk2 = (a[l][None] for a in (lambda_q1, lambda_k1, lambda_q2, lambda_k2))

    def gating_params(chunk):
        reps = GMLP_CHUNK // chunk
        ws_t = jnp.tile(w_spatial[l][:, :chunk, :chunk], (1, reps, reps))
        bs_t = jnp.tile(b_spatial[l][:, :chunk], (1, reps)).T
        return ws_t, bs_t

    xp = x_prompt.reshape(n_seq * seq, D_MODEL)
    q2, kt, ktb, v, vb, mix, _ = _proj(
        xp, seq, 512, GMLP_CHUNK, nm, w_in_b, qg, kg, gvg, *gating_params(GMLP_CHUNK))
    att = _prompt_attn(q2, ktb, vb, lq1, lk1, lq2, lk2, sg,
                       n_seq=n_seq, seq=seq, tq=256, lam_init=lam_init)
    k_prompt = kt.reshape(n_seq, N_HEADS, 2, QK_DIM, seq).transpose(0, 4, 1, 2, 3)[None]
    v_prompt = v.reshape(1, n_seq, seq, N_HEADS, V_DIM)

    rows_s = n_dec * n_new
    xs = x_sample.reshape(rows_s, D_MODEL)
    q2s, kts, ktbs, vs, vbs, mixs, gvs = _proj(
        xs, rows_s, rows_s, n_new, nm, w_in_b, qg, kg, gvg, *gating_params(n_new))
    qpad = jnp.pad(q2s.reshape(2, n_dec, n_new, 512).astype(F32),
                   ((0, 0), (0, 0), (0, Q_PAD - n_new), (0, 0)))
    n_phys = cache_k.shape[1]
    ckt = cache_k[l].transpose(0, 2, 3, 4, 1).reshape(n_phys, 512, PAGE)
    cv = cache_v[l].reshape(n_phys, PAGE * N_HEADS, V_DIM)
    y_prompt, att_s = _post_decode(page_table, xp, att, mix, wo_b, nf, wu_b, wd_b,
                                   qpad, ktbs, vbs, ckt, cv, lq1, lk1, lq2, lk2, sg,
                                   n_new=n_new, lam_init=lam_init)
    y_prompt = y_prompt.reshape(n_seq, seq, D_MODEL)
    att_s = att_s[:, :n_new].reshape(rows_s, ATT_WIDTH).astype(BF16)
    y_sample = _post(xs, att_s, mixs, wo_b, nf, wu_b, wd_b, rows_s).reshape(n_dec, n_new, D_MODEL)
    k_sample = kts[0].T.reshape(1, n_dec, n_new, N_HEADS, 2, QK_DIM)
    v_sample = vs.reshape(1, n_dec, n_new, N_HEADS, V_DIM)
    gv_sample = gvs.reshape(1, n_dec, n_new, GMLP_WIDTH)

    return (y_prompt, y_sample, k_prompt, v_prompt, k_sample, v_sample, gv_sample)
```

```python
import functools
import math

import jax
import jax.numpy as jnp
from jax import lax
from jax.experimental import pallas as pl
from jax.experimental.pallas import tpu as pltpu

F32 = jnp.float32
BF16 = jnp.bfloat16

D_MODEL = 1024
N_HEADS = 4
QK_DIM = 64
V_DIM = 128
HEAD_COLS = 2 * QK_DIM
ATT_WIDTH = N_HEADS * V_DIM
GMLP_WIDTH = 512
N_GROUPS = 4
GROUP_CH = 128
GMLP_CHUNK = 128
D_FF = 4096
PAGE = 128
EPS = 1e-6
LANES = 128
VMEM_LIMIT = 56 * 1024 * 1024

GELU_C = math.sqrt(2.0 / math.pi)
LOG2E = math.log2(math.e)


def _gelu(x):
    return 0.5 * x * (1.0 + jnp.tanh(GELU_C * (x + 0.044715 * (x * x * x))))


def _const_spec(shape):
    nd = len(shape)
    return pl.BlockSpec(shape, lambda *_: (0,) * nd, pipeline_mode=pl.Buffered(1))


def _half_group_rms(t):
    lane = lax.broadcasted_iota(jnp.int32, (1, LANES), 1)
    lo = lane < QK_DIM
    outs = []
    for j in range(t.shape[1] // LANES):
        c = t[:, j * LANES:(j + 1) * LANES]
        c2 = c * c
        s_lo = jnp.sum(jnp.where(lo, c2, 0.0), axis=-1, keepdims=True)
        s_hi = jnp.sum(jnp.where(lo, 0.0, c2), axis=-1, keepdims=True)
        r = jnp.where(lo, lax.rsqrt(s_lo * (1.0 / QK_DIM) + EPS),
                      lax.rsqrt(s_hi * (1.0 / QK_DIM) + EPS))
        outs.append(c * r)
    return jnp.concatenate(outs, axis=-1)


def _proj_kernel(x_ref, nm_ref, w_ref, qg_ref, kg_ref, gvg_ref, ws_ref, bs_ref,
                 q_ref, kt_ref, ktb_ref, v_ref, vb_ref, mix_ref, gv_ref, *, chunk):
    tm = x_ref.shape[0]
    x = x_ref[...]
    ms = jnp.mean(x * x, axis=-1, keepdims=True)
    xn = (x * lax.rsqrt(ms + EPS) * nm_ref[...]).astype(BF16)

    def seg(i):
        return jnp.dot(xn, w_ref[:, i * 512:(i + 1) * 512], preferred_element_type=F32)

    lane = lax.broadcasted_iota(jnp.int32, (1, 512), 1)
    first_comp = (lane % HEAD_COLS) < QK_DIM

    qn = _half_group_rms(seg(0)) * (qg_ref[...] * (QK_DIM ** -0.5 * LOG2E))
    q_ref[0] = jnp.where(first_comp, qn, 0.0).astype(BF16)
    q_ref[1] = jnp.where(first_comp, 0.0, qn).astype(BF16)

    kn = _half_group_rms(seg(1)) * kg_ref[...]
    knt = kn.T
    kt_ref[0] = knt
    ktb_ref[0] = knt.astype(BF16)

    v = seg(2)
    for h in range(N_HEADS):
        v_ref[pl.ds(h, tm, stride=N_HEADS), :] = v[:, h * V_DIM:(h + 1) * V_DIM]
    vb_ref[...] = v.astype(BF16)

    u = _gelu(seg(3))
    gr = _gelu(seg(4))
    gvs = []
    for g in range(N_GROUPS):
        c = gr[:, g * GROUP_CH:(g + 1) * GROUP_CH]
        r = lax.rsqrt(jnp.mean(c * c, axis=-1, keepdims=True) + EPS)
        gvs.append(c * r * gvg_ref[:, g * GROUP_CH:(g + 1) * GROUP_CH])
    gv_ref[...] = jnp.concatenate(gvs, axis=-1)

    row = lax.broadcasted_iota(jnp.int32, (GMLP_CHUNK, GMLP_CHUNK), 0)
    col = lax.broadcasted_iota(jnp.int32, (GMLP_CHUNK, GMLP_CHUNK), 1)
    keep = (col <= row) & ((row // chunk) == (col // chunk))
    n_blk = tm // GMLP_CHUNK
    for g in range(N_GROUPS):
        wm = jnp.where(keep, ws_ref[g], 0.0).astype(BF16)
        rhs = jnp.concatenate(
            [gvs[g][b * GMLP_CHUNK:(b + 1) * GMLP_CHUNK].astype(BF16) for b in range(n_blk)],
            axis=-1)
        s = jnp.dot(wm, rhs, preferred_element_type=F32)
        bias = bs_ref[:, g:g + 1]
        for b in range(n_blk):
            sb = s[:, b * GROUP_CH:(b + 1) * GROUP_CH] + bias
            ub = u[b * GMLP_CHUNK:(b + 1) * GMLP_CHUNK, g * GROUP_CH:(g + 1) * GROUP_CH]
            mix_ref[b * GMLP_CHUNK:(b + 1) * GMLP_CHUNK,
                    g * GROUP_CH:(g + 1) * GROUP_CH] = (ub * sb).astype(BF16)


def _proj(x2d, seq, tm, chunk, nm, w_in_b, qg, kg, gvg, ws_t, bs_t):
    rows = x2d.shape[0]
    n_seq = rows // seq
    per_seq = seq // tm
    grid = (rows // tm,)
    row_spec = lambda w: pl.BlockSpec((tm, w), lambda i: (i, 0))
    out_shape = (
        jax.ShapeDtypeStruct((2, rows, 512), BF16),
        jax.ShapeDtypeStruct((n_seq, 512, seq), F32),
        jax.ShapeDtypeStruct((n_seq, 512, seq), BF16),
        jax.ShapeDtypeStruct((rows * N_HEADS, V_DIM), F32),
        jax.ShapeDtypeStruct((rows, 512), BF16),
        jax.ShapeDtypeStruct((rows, 512), BF16),
        jax.ShapeDtypeStruct((rows, 512), F32),
    )
    kt_spec = pl.BlockSpec((1, 512, tm), lambda i: (i // per_seq, 0, i % per_seq))
    return pl.pallas_call(
        functools.partial(_proj_kernel, chunk=chunk),
        grid=grid,
        in_specs=[
            row_spec(D_MODEL),
            _const_spec((1, D_MODEL)),
            _const_spec((D_MODEL, 2560)),
            _const_spec((1, 512)),
            _const_spec((1, 512)),
            _const_spec((1, 512)),
            _const_spec((N_GROUPS, GMLP_CHUNK, GMLP_CHUNK)),
            _const_spec((GMLP_CHUNK, N_GROUPS)),
        ],
        out_specs=(
            pl.BlockSpec((2, tm, 512), lambda i: (0, i, 0)),
            kt_spec, kt_spec,
            pl.BlockSpec((tm * N_HEADS, V_DIM), lambda i: (i, 0)),
            row_spec(512), row_spec(512), row_spec(512),
        ),
        out_shape=out_shape,
        compiler_params=pltpu.CompilerParams(
            dimension_semantics=("arbitrary",), vmem_limit_bytes=VMEM_LIMIT),
        name="proj",
    )(x2d, nm, w_in_b, qg, kg, gvg, ws_t, bs_t)


def _lambda(lq1_ref, lk1_ref, lq2_ref, lk2_ref, lam_init):
    a = jnp.sum(lq1_ref[...] * lk1_ref[...], axis=-1, keepdims=True)
    b = jnp.sum(lq2_ref[...] * lk2_ref[...], axis=-1, keepdims=True)
    return jnp.exp(a) - jnp.exp(b) + lam_init


def _sub_ln(att, gain, lam_init):
    r = lax.rsqrt(jnp.mean(att * att, axis=-1, keepdims=True) + EPS)
    return att * r * gain * (1.0 - lam_init)


def _attn_kernel(q_ref, kt_ref, v_ref, lq1_ref, lk1_ref, lq2_ref, lk2_ref, sg_ref,
                 o_ref, *, tq, nq, lam_init):
    pair = pl.program_id(2)

    def scores(c):
        q = jnp.concatenate([q_ref[0, c * tq:(c + 1) * tq, :], q_ref[1, c * tq:(c + 1) * tq, :]],
                            axis=0)
        row = lax.broadcasted_iota(jnp.int32, (2 * tq, tq), 0) % tq
        col = lax.broadcasted_iota(jnp.int32, (2 * tq, tq), 1)
        s_d = jnp.dot(q, kt_ref[0, :, c * tq:(c + 1) * tq], preferred_element_type=F32)
        s_d = jnp.where(col <= row, s_d, -jnp.inf)
        s_f = jnp.dot(q, kt_ref[0, :, :c * tq], preferred_element_type=F32) if c else None
        return s_d, s_f

    def finish(c, s_d, s_f):
        lo, hi = c * tq, (c + 1) * tq
        m = jnp.max(s_d, axis=-1, keepdims=True)
        if c:
            m = jnp.maximum(m, jnp.max(s_f, axis=-1, keepdims=True))
        p_d = jnp.exp2(s_d - m)
        l = jnp.sum(p_d, axis=-1, keepdims=True)
        acc = jnp.dot(p_d.astype(BF16), v_ref[lo:hi, :], preferred_element_type=F32)
        if c:
            p_f = jnp.exp2(s_f - m)
            l = l + jnp.sum(p_f, axis=-1, keepdims=True)
            acc = acc + jnp.dot(p_f.astype(BF16), v_ref[:lo, :], preferred_element_type=F32)
        lam = _lambda(lq1_ref, lk1_ref, lq2_ref, lk2_ref, lam_init)
        o = acc / l
        att = o[:tq] - lam * o[tq:]
        o_ref[lo:hi, :] = _sub_ln(att, sg_ref[...], lam_init).astype(o_ref.dtype)

    for k in range(nq // 2):
        @pl.when(pair == k)
        def _pair(k=k):
            a, b = k, nq - 1 - k
            s_a = scores(a)
            s_b = scores(b)
            finish(a, *s_a)
            finish(b, *s_b)


def _prompt_attn(q2, ktb, vb, lq1, lk1, lq2, lk2, sg, *, n_seq, seq, tq, lam_init):
    nq = seq // tq
    vec = lambda n: _const_spec((1, n))
    return pl.pallas_call(
        functools.partial(_attn_kernel, tq=tq, nq=nq, lam_init=lam_init),
        grid=(n_seq, N_HEADS, nq // 2),
        in_specs=[
            pl.BlockSpec((2, seq, HEAD_COLS), lambda b, h, i: (0, b, h)),
            pl.BlockSpec((1, HEAD_COLS, seq), lambda b, h, i: (b, h, 0)),
            pl.BlockSpec((seq, V_DIM), lambda b, h, i: (b, h)),
            vec(QK_DIM), vec(QK_DIM), vec(QK_DIM), vec(QK_DIM), vec(V_DIM),
        ],
        out_specs=pl.BlockSpec((seq, V_DIM), lambda b, h, i: (b, h)),
        out_shape=jax.ShapeDtypeStruct((n_seq * seq, ATT_WIDTH), BF16),
        compiler_params=pltpu.CompilerParams(
            dimension_semantics=("arbitrary", "arbitrary", "arbitrary"),
            vmem_limit_bytes=VMEM_LIMIT),
        name="prompt_attn",
    )(q2, ktb, vb, lq1, lk1, lq2, lk2, sg)


Q_PAD = 8


GRP = 2 * Q_PAD


def _decode_init(b, q_ref, ktn_ref, vn_ref, qbd_sc, m_sc, l_sc, acc_sc, p_sc, alpha_sc, *, n_new):
    grp = GRP
    p_sc[...] = jnp.zeros(p_sc.shape, p_sc.dtype)
    alpha_sc[...] = jnp.ones(alpha_sc.shape, alpha_sc.dtype)
    lane = lax.broadcasted_iota(jnp.int32, (Q_PAD, 512), 1)
    for h in range(N_HEADS):
        own = (lane // HEAD_COLS) == h
        for c in range(2):
            r0 = h * grp + c * Q_PAD
            qbd_sc[r0:r0 + Q_PAD, :] = jnp.where(own, q_ref[c, 0], 0.0)
    s = jnp.dot(qbd_sc[...].astype(BF16), ktn_ref[0], preferred_element_type=F32)
    t = lax.broadcasted_iota(jnp.int32, s.shape, 0) % Q_PAD
    col = lax.broadcasted_iota(jnp.int32, s.shape, 1)
    ok = ((col // n_new) == b) & ((col % n_new) <= t)
    s = jnp.where(ok, s, -jnp.inf)
    m = jnp.max(s, axis=-1, keepdims=True)
    p = jnp.exp2(s - m)
    m_sc[...] = m
    l_sc[...] = jnp.sum(p, axis=-1, keepdims=True)
    pb = p.astype(BF16)
    for h in range(N_HEADS):
        acc_sc[h * grp:(h + 1) * grp, :] = jnp.dot(
            pb[h * grp:(h + 1) * grp], vn_ref[:, h * V_DIM:(h + 1) * V_DIM],
            preferred_element_type=F32)


def _decode_scores(k_refs, qbd_sc, m_sc, l_sc, p_sc, alpha_sc):
    kt = jnp.concatenate(
        [r[...].astype(BF16) for r in k_refs], axis=-1)
    s = jnp.dot(qbd_sc[...].astype(BF16), kt, preferred_element_type=F32)
    m_old = m_sc[...]
    m_new = jnp.maximum(m_old, jnp.max(s, axis=-1, keepdims=True))
    p = jnp.exp2(s - m_new)
    alpha = jnp.exp2(m_old - m_new)
    l_sc[...] = alpha * l_sc[...] + jnp.sum(p, axis=-1, keepdims=True)
    m_sc[...] = m_new
    p_sc[...] = p.astype(BF16)
    alpha_sc[...] = alpha


def _decode_values(v_refs, p_sc, alpha_sc, acc_sc):
    grp = GRP
    pb = p_sc[...]
    alpha = alpha_sc[...]
    for hp in range(N_HEADS // 2):
        vh = jnp.concatenate(
            [jnp.concatenate(
                [r[pl.ds(2 * hp + e, PAGE, stride=N_HEADS), :].astype(BF16) for e in range(2)],
                axis=-1) for r in v_refs], axis=0)
        pv = jnp.dot(pb[2 * hp * grp:(2 * hp + 2) * grp], vh, preferred_element_type=F32)
        for e in range(2):
            sl = slice((2 * hp + e) * grp, (2 * hp + e + 1) * grp)
            acc_sc[sl, :] = alpha[sl] * acc_sc[sl, :] + pv[e * grp:(e + 1) * grp,
                                                           e * V_DIM:(e + 1) * V_DIM]


def _decode_finish(lam_refs, sg_ref, o_ref, l_sc, acc_sc, *, lam_init):
    grp = GRP
    lam = _lambda(*lam_refs, lam_init)
    o = acc_sc[...] / l_sc[...]
    for h in range(N_HEADS):
        att = o[h * grp:h * grp + Q_PAD] - lam * o[h * grp + Q_PAD:(h + 1) * grp]
        o_ref[0, :, h * V_DIM:(h + 1) * V_DIM] = _sub_ln(
            att, sg_ref[...], lam_init).astype(o_ref.dtype)


def _mix_out(x_ref, att_ref, mix_ref, wo_ref, nf_ref, o_ref):
    o_ref[...] = (x_ref[...]
                  + jnp.dot(att_ref[...], wo_ref[:ATT_WIDTH, :], preferred_element_type=F32)
                  + jnp.dot(mix_ref[...], wo_ref[ATT_WIDTH:, :], preferred_element_type=F32))
    h = o_ref[...]
    ms = jnp.mean(h * h, axis=-1, keepdims=True)
    return (h * lax.rsqrt(ms + EPS) * nf_ref[...]).astype(BF16)


def _ffn_cols(hn, wu, wd):
    up = jnp.dot(hn, wu, preferred_element_type=F32)
    return jnp.dot(jnp.square(jnp.maximum(up, 0.0)).astype(BF16), wd, preferred_element_type=F32)


FF_BLK = 1024


def _post_kernel(x_ref, att_ref, mix_ref, wo_ref, nf_ref, wu_ref, wd_ref, o_ref):
    hn = _mix_out(x_ref, att_ref, mix_ref, wo_ref, nf_ref, o_ref)
    out = o_ref[...]
    for c in range(D_FF // FF_BLK):
        cols = slice(c * FF_BLK, (c + 1) * FF_BLK)
        out = out + _ffn_cols(hn, wu_ref[:, cols], wd_ref[cols, :])
    o_ref[...] = out


def _post(x2d, att, mix, wo_b, nf, wu_b, wd_b, tm):
    rows = x2d.shape[0]
    row_spec = lambda w: pl.BlockSpec((tm, w), lambda i: (i, 0))
    return pl.pallas_call(
        _post_kernel,
        grid=(rows // tm,),
        in_specs=[
            row_spec(D_MODEL), row_spec(ATT_WIDTH), row_spec(GMLP_WIDTH),
            _const_spec((D_MODEL, D_MODEL)),
            _const_spec((1, D_MODEL)),
            _const_spec((D_MODEL, D_FF)),
            _const_spec((D_FF, D_MODEL)),
        ],
        out_specs=row_spec(D_MODEL),
        out_shape=jax.ShapeDtypeStruct((rows, D_MODEL), F32),
        compiler_params=pltpu.CompilerParams(
            dimension_semantics=("arbitrary",), vmem_limit_bytes=VMEM_LIMIT),
        name="post",
    )(x2d, att, mix, wo_b, nf, wu_b, wd_b)


N_SLOTS = 4
AHEAD = N_SLOTS - 2
SUB = 2


def _post_decode_kernel(pt_ref, x_ref, att_ref, mix_ref, wo_ref, nf_ref, wu_ref, wd_ref,
                        q_ref, ktn_ref, vn_ref, lq1_ref, lk1_ref, lq2_ref, lk2_ref, sg_ref,
                        ck_hbm, cv_hbm, o_ref, os_ref,
                        hn_sc, a_sc, qbd_sc, m_sc, l_sc, acc_sc, p_sc, alpha_sc, kbuf, vbuf, sem,
                        *, n_pages, n_new, ff_blk, lam_init):
    i = pl.program_id(0)
    j = pl.program_id(1)
    n_steps = pl.num_programs(1)
    grp_per_seq = n_steps * SUB
    n_groups = pl.num_programs(0) * grp_per_seq
    g0 = (i * n_steps + j) * SUB

    def group_copies(g, wrap=False):
        s = g % N_SLOTS
        if wrap:
            g = g % n_groups
        seq, k = g // grp_per_seq, g % grp_per_seq
        cps = []
        for p in range(n_pages):
            page = pt_ref[seq, k * n_pages + p]
            cps.append(pltpu.make_async_copy(ck_hbm.at[page], kbuf.at[s, p], sem.at[s, 0]))
            cps.append(pltpu.make_async_copy(cv_hbm.at[page], vbuf.at[s, p], sem.at[s, 1]))
        return cps

    def k_pages(g):
        return [kbuf.at[g % N_SLOTS, p] for p in range(n_pages)]

    def v_pages(g):
        return [vbuf.at[g % N_SLOTS, p] for p in range(n_pages)]

    @pl.when(g0 == 0)
    def _():
        for g in range(AHEAD):
            for cp in group_copies(g):
                cp.start()
        vbuf[N_SLOTS - 1] = jnp.zeros(vbuf.shape[1:], vbuf.dtype)

    for cp in group_copies(g0 + AHEAD, wrap=True):
        cp.start()
    for cp in group_copies(g0):
        cp.wait()

    @pl.when(j == 0)
    def _():
        hn_sc[...] = _mix_out(x_ref, att_ref, mix_ref, wo_ref, nf_ref, o_ref)
        _decode_init(i, q_ref, ktn_ref, vn_ref, qbd_sc, m_sc, l_sc, acc_sc, p_sc, alpha_sc,
                     n_new=n_new)

    c0 = pl.multiple_of(j * ff_blk, ff_blk)
    for u in range(SUB):
        g = g0 + u
        if u:
            for cp in group_copies(g + AHEAD, wrap=True):
                cp.start()
            for cp in group_copies(g):
                cp.wait()
        _decode_values(v_pages(g + N_SLOTS - 1), p_sc, alpha_sc, acc_sc)
        _decode_scores(k_pages(g), qbd_sc, m_sc, l_sc, p_sc, alpha_sc)
        if u == 0:
            up = jnp.dot(hn_sc[...], wu_ref[:, pl.ds(c0, ff_blk)], preferred_element_type=F32)
            a_sc[...] = jnp.square(jnp.maximum(up, 0.0)).astype(BF16)
        else:
            o_ref[...] += jnp.dot(a_sc[...], wd_ref[pl.ds(c0, ff_blk), :],
                                  preferred_element_type=F32)

    @pl.when(j == n_steps - 1)
    def _():
        _decode_values(v_pages(g0 + SUB - 1), p_sc, alpha_sc, acc_sc)
        _decode_finish((lq1_ref, lk1_ref, lq2_ref, lk2_ref), sg_ref, os_ref, l_sc, acc_sc,
                       lam_init=lam_init)

    @pl.when(g0 + SUB == n_groups)
    def _():
        for g in range(AHEAD):
            for cp in group_copies(n_groups + g, wrap=True):
                cp.wait()


def _post_decode(page_table, x2d, att, mix, wo_b, nf, wu_b, wd_b,
                 qpad, ktn, vn, ckt, cv, lq1, lk1, lq2, lk2, sg, *, n_new, lam_init):
    rows = x2d.shape[0]
    n_dec, n_tab = page_table.shape
    tm = rows // n_dec
    steps = 8
    ff_blk = D_FF // steps
    n_pages = n_tab // (steps * SUB)
    q_rows = N_HEADS * 2 * Q_PAD
    const = lambda shape: pl.BlockSpec(shape, lambda i, j, pt: (0,) * len(shape),
                                       pipeline_mode=pl.Buffered(1))
    row_spec = lambda w: pl.BlockSpec((tm, w), lambda i, j, pt: (i, 0))
    page_buf = pltpu.VMEM((N_SLOTS, n_pages, 512, PAGE), F32)
    grid_spec = pltpu.PrefetchScalarGridSpec(
        num_scalar_prefetch=1,
        grid=(n_dec, steps),
        in_specs=[
            row_spec(D_MODEL), row_spec(ATT_WIDTH), row_spec(GMLP_WIDTH),
            const((D_MODEL, D_MODEL)), const((1, D_MODEL)),
            const((D_MODEL, D_FF)), const((D_FF, D_MODEL)),
            pl.BlockSpec((2, 1, Q_PAD, 512), lambda i, j, pt: (0, i, 0, 0)),
            const((1, 512, n_dec * n_new)), const((n_dec * n_new, 512)),
            const((1, QK_DIM)), const((1, QK_DIM)), const((1, QK_DIM)), const((1, QK_DIM)),
            const((1, V_DIM)),
            pl.BlockSpec(memory_space=pl.ANY),
            pl.BlockSpec(memory_space=pl.ANY),
        ],
        out_specs=(row_spec(D_MODEL),
                   pl.BlockSpec((1, Q_PAD, ATT_WIDTH), lambda i, j, pt: (i, 0, 0))),
        scratch_shapes=[
            pltpu.VMEM((tm, D_MODEL), BF16),
            pltpu.VMEM((tm, ff_blk), BF16),
            pltpu.VMEM((q_rows, 512), F32),
            pltpu.VMEM((q_rows, 1), F32),
            pltpu.VMEM((q_rows, 1), F32),
            pltpu.VMEM((q_rows, V_DIM), F32),
            pltpu.VMEM((q_rows, n_pages * PAGE), BF16),
            pltpu.VMEM((q_rows, 1), F32),
            page_buf, page_buf,
            pltpu.SemaphoreType.DMA((N_SLOTS, 2)),
        ],
    )
    return pl.pallas_call(
        functools.partial(_post_decode_kernel, n_pages=n_pages, n_new=n_new, ff_blk=ff_blk,
                          lam_init=lam_init),
        grid_spec=grid_spec,
        out_shape=(jax.ShapeDtypeStruct((rows, D_MODEL), F32),
                   jax.ShapeDtypeStruct((n_dec, Q_PAD, ATT_WIDTH), F32)),
        compiler_params=pltpu.CompilerParams(
            dimension_semantics=("arbitrary", "arbitrary"), vmem_limit_bytes=VMEM_LIMIT),
        name="post_decode",
    )(page_table, x2d, att, mix, wo_b, nf, wu_b, wd_b, qpad, ktn, vn, lq1, lk1, lq2, lk2, sg,
      ckt, cv)


def kernel(x_prompt, x_sample, cache_k, cache_v, page_table, norm_mix, w_in, q_gain, k_gain,
           lambda_q1, lambda_k1, lambda_q2, lambda_k2, subln_gain, gv_gain, w_spatial,
           b_spatial, w_out, norm_ffn, w_up, w_down):
    depth = w_in.shape[0]
    assert depth == 1, "single-layer step"
    n_seq, seq, _ = x_prompt.shape
    n_dec, n_new, _ = x_sample.shape
    l = 0
    lam_init = 0.8 - 0.6 * math.exp(-0.3 * l)

    w_in_b = w_in[l].astype(BF16)
    wo_b = w_out[l].astype(BF16)
    wu_b = w_up[l].astype(BF16)
    wd_b = w_down[l].astype(BF16)
    qg = jnp.tile(q_gain[l], 2 * N_HEADS)[None]
    kg = jnp.tile(k_gain[l], 2 * N_HEADS)[None]
    gvg = gv_gain[l].reshape(1, GMLP_WIDTH)
    nm = norm_mix[l][None]
    nf = norm_ffn[l][None]
    sg = subln_gain[l][None]
    lq1, lk1, lq2, lk2 = (a[l][None] for a in (lambda_q1, lambda_k1, lambda_q2, lambda_k2))

    def gating_params(chunk):
        reps = GMLP_CHUNK // chunk
        ws_t = jnp.tile(w_spatial[l][:, :chunk, :chunk], (1, reps, reps))
        bs_t = jnp.tile(b_spatial[l][:, :chunk], (1, reps)).T
        return ws_t, bs_t

    xp = x_prompt.reshape(n_seq * seq, D_MODEL)
    q2, kt, ktb, v, vb, mix, _ = _proj(
        xp, seq, 512, GMLP_CHUNK, nm, w_in_b, qg, kg, gvg, *gating_params(GMLP_CHUNK))
    att = _prompt_attn(q2, ktb, vb, lq1, lk1, lq2, lk2, sg,
                       n_seq=n_seq, seq=seq, tq=256, lam_init=lam_init)
    k_prompt = kt.reshape(n_seq, N_HEADS, 2, QK_DIM, seq).transpose(0, 4, 1, 2, 3)[None]
    v_prompt = v.reshape(1, n_seq, seq, N_HEADS, V_DIM)

    rows_s = n_dec * n_new
    xs = x_sample.reshape(rows_s, D_MODEL)
    q2s, kts, ktbs, vs, vbs, mixs, gvs = _proj(
        xs, rows_s, rows_s, n_new, nm, w_in_b, qg, kg, gvg, *gating_params(n_new))
    qpad = jnp.pad(q2s.reshape(2, n_dec, n_new, 512).astype(F32),
                   ((0, 0), (0, 0), (0, Q_PAD - n_new), (0, 0)))
    n_phys = cache_k.shape[1]
    ckt = cache_k[l].transpose(0, 2, 3, 4, 1).reshape(n_phys, 512, PAGE)
    cv = cache_v[l].reshape(n_phys, PAGE * N_HEADS, V_DIM)
    y_prompt, att_s = _post_decode(page_table, xp, att, mix, wo_b, nf, wu_b, wd_b,
                                   qpad, ktbs, vbs, ckt, cv, lq1, lk1, lq2, lk2, sg,
                                   n_new=n_new, lam_init=lam_init)
    y_prompt = y_prompt.reshape(n_seq, seq, D_MODEL)
    att_s = att_s[:, :n_new].reshape(rows_s, ATT_WIDTH).astype(BF16)
    y_sample = _post(xs, att_s, mixs, wo_b, nf, wu_b, wd_b, rows_s).reshape(n_dec, n_new, D_MODEL)
    k_sample = kts[0].T.reshape(1, n_dec, n_new, N_HEADS, 2, QK_DIM)
    v_sample = vs.reshape(1, n_dec, n_new, N_HEADS, V_DIM)
    gv_sample = gvs.reshape(1, n_dec, n_new, GMLP_WIDTH)

    return (y_prompt, y_sample, k_prompt, v_prompt, k_sample, v_sample, gv_sample)
```

```python
import functools
import math

import jax
import jax.numpy as jnp
from jax import lax
from jax.experimental import pallas as pl
from jax.experimental.pallas import tpu as pltpu

F32 = jnp.float32
BF16 = jnp.bfloat16

D_MODEL = 1024
N_HEADS = 4
QK_DIM = 64
V_DIM = 128
HEAD_COLS = 2 * QK_DIM
ATT_WIDTH = N_HEADS * V_DIM
GMLP_WIDTH = 512
N_GROUPS = 4
GROUP_CH = 128
GMLP_CHUNK = 128
D_FF = 4096
PAGE = 128
EPS = 1e-6
LANES = 128
VMEM_LIMIT = 56 * 1024 * 1024

GELU_C = math.sqrt(2.0 / math.pi)
LOG2E = math.log2(math.e)


def _gelu(x):
    return 0.5 * x * (1.0 + jnp.tanh(GELU_C * (x + 0.044715 * (x * x * x))))


def _const_spec(shape):
    nd = len(shape)
    return pl.BlockSpec(shape, lambda *_: (0,) * nd, pipeline_mode=pl.Buffered(1))


def _half_group_rms(t):
    lane = lax.broadcasted_iota(jnp.int32, (1, LANES), 1)
    lo = lane < QK_DIM
    outs = []
    for j in range(t.shape[1] // LANES):
        c = t[:, j * LANES:(j + 1) * LANES]
        c2 = c * c
        s_lo = jnp.sum(jnp.where(lo, c2, 0.0), axis=-1, keepdims=True)
        s_hi = jnp.sum(jnp.where(lo, 0.0, c2), axis=-1, keepdims=True)
        r = jnp.where(lo, lax.rsqrt(s_lo * (1.0 / QK_DIM) + EPS),
                      lax.rsqrt(s_hi * (1.0 / QK_DIM) + EPS))
        outs.append(c * r)
    return jnp.concatenate(outs, axis=-1)


def _proj_kernel(x_ref, nm_ref, w_ref, qg_ref, kg_ref, gvg_ref, ws_ref, bs_ref,
                 q_ref, kt_ref, ktb_ref, v_ref, vb_ref, mix_ref, gv_ref, *, chunk):
    tm = x_ref.shape[0]
    x = x_ref[...]
    ms = jnp.mean(x * x, axis=-1, keepdims=True)
    xn = (x * lax.rsqrt(ms + EPS) * nm_ref[...]).astype(BF16)

    def seg(i):
        return jnp.dot(xn, w_ref[:, i * 512:(i + 1) * 512], preferred_element_type=F32)

    lane = lax.broadcasted_iota(jnp.int32, (1, 512), 1)
    first_comp = (lane % HEAD_COLS) < QK_DIM

    qn = _half_group_rms(seg(0)) * (qg_ref[...] * (QK_DIM ** -0.5 * LOG2E))
    q_ref[0] = jnp.where(first_comp, qn, 0.0).astype(BF16)
    q_ref[1] = jnp.where(first_comp, 0.0, qn).astype(BF16)

    kn = _half_group_rms(seg(1)) * kg_ref[...]
    knt = kn.T
    kt_ref[0] = knt
    ktb_ref[0] = knt.astype(BF16)

    v = seg(2)
    for h in range(N_HEADS):
        v_ref[pl.ds(h, tm, stride=N_HEADS), :] = v[:, h * V_DIM:(h + 1) * V_DIM]
    vb_ref[...] = v.astype(BF16)

    u = _gelu(seg(3))
    gr = _gelu(seg(4))
    gvs = []
    for g in range(N_GROUPS):
        c = gr[:, g * GROUP_CH:(g + 1) * GROUP_CH]
        r = lax.rsqrt(jnp.mean(c * c, axis=-1, keepdims=True) + EPS)
        gvs.append(c * r * gvg_ref[:, g * GROUP_CH:(g + 1) * GROUP_CH])
    gv_ref[...] = jnp.concatenate(gvs, axis=-1)

    row = lax.broadcasted_iota(jnp.int32, (GMLP_CHUNK, GMLP_CHUNK), 0)
    col = lax.broadcasted_iota(jnp.int32, (GMLP_CHUNK, GMLP_CHUNK), 1)
    keep = (col <= row) & ((row // chunk) == (col // chunk))
    sel = jnp.where(col == row % chunk, 1.0, 0.0).astype(BF16)
    n_blk = tm // GMLP_CHUNK
    for g in range(N_GROUPS):
        wg = ws_ref[g].astype(BF16)
        if chunk < GMLP_CHUNK:
            wg = jnp.dot(sel, wg, preferred_element_type=F32).astype(BF16)
            wg = lax.dot_general(wg, sel, (((1,), (1,)), ((), ())),
                                 preferred_element_type=F32).astype(BF16)
        wm = jnp.where(keep, wg, jnp.zeros_like(wg))
        rhs = jnp.concatenate(
            [gvs[g][b * GMLP_CHUNK:(b + 1) * GMLP_CHUNK].astype(BF16) for b in range(n_blk)],
            axis=-1)
        s = jnp.dot(wm, rhs, preferred_element_type=F32)
        bias = bs_ref[:, g:g + 1]
        for b in range(n_blk):
            sb = s[:, b * GROUP_CH:(b + 1) * GROUP_CH] + bias
            ub = u[b * GMLP_CHUNK:(b + 1) * GMLP_CHUNK, g * GROUP_CH:(g + 1) * GROUP_CH]
            mix_ref[b * GMLP_CHUNK:(b + 1) * GMLP_CHUNK,
                    g * GROUP_CH:(g + 1) * GROUP_CH] = (ub * sb).astype(BF16)


def _proj(x2d, seq, tm, chunk, nm, w_in_b, qg, kg, gvg, ws_t, bs_t):
    rows = x2d.shape[0]
    n_seq = rows // seq
    per_seq = seq // tm
    grid = (rows // tm,)
    row_spec = lambda w: pl.BlockSpec((tm, w), lambda i: (i, 0))
    out_shape = (
        jax.ShapeDtypeStruct((2, rows, 512), BF16),
        jax.ShapeDtypeStruct((n_seq, 512, seq), F32),
        jax.ShapeDtypeStruct((n_seq, 512, seq), BF16),
        jax.ShapeDtypeStruct((rows * N_HEADS, V_DIM), F32),
        jax.ShapeDtypeStruct((rows, 512), BF16),
        jax.ShapeDtypeStruct((rows, 512), BF16),
        jax.ShapeDtypeStruct((rows, 512), F32),
    )
    kt_spec = pl.BlockSpec((1, 512, tm), lambda i: (i // per_seq, 0, i % per_seq))
    return pl.pallas_call(
        functools.partial(_proj_kernel, chunk=chunk),
        grid=grid,
        in_specs=[
            row_spec(D_MODEL),
            _const_spec((1, D_MODEL)),
            _const_spec((D_MODEL, 2560)),
            _const_spec((1, 512)),
            _const_spec((1, 512)),
            _const_spec((1, 512)),
            _const_spec((N_GROUPS, GMLP_CHUNK, GMLP_CHUNK)),
            _const_spec((GMLP_CHUNK, N_GROUPS)),
        ],
        out_specs=(
            pl.BlockSpec((2, tm, 512), lambda i: (0, i, 0)),
            kt_spec, kt_spec,
            pl.BlockSpec((tm * N_HEADS, V_DIM), lambda i: (i, 0)),
            row_spec(512), row_spec(512), row_spec(512),
        ),
        out_shape=out_shape,
        compiler_params=pltpu.CompilerParams(
            dimension_semantics=("arbitrary",), vmem_limit_bytes=VMEM_LIMIT),
        name="proj",
    )(x2d, nm, w_in_b, qg, kg, gvg, ws_t, bs_t)


def _lambda(lq1_ref, lk1_ref, lq2_ref, lk2_ref, lam_init):
    a = jnp.sum(lq1_ref[...] * lk1_ref[...], axis=-1, keepdims=True)
    b = jnp.sum(lq2_ref[...] * lk2_ref[...], axis=-1, keepdims=True)
    return jnp.exp(a) - jnp.exp(b) + lam_init


def _sub_ln(att, gain, lam_init):
    r = lax.rsqrt(jnp.mean(att * att, axis=-1, keepdims=True) + EPS)
    return att * r * gain * (1.0 - lam_init)


def _attn_kernel(q_ref, kt_ref, v_ref, lq1_ref, lk1_ref, lq2_ref, lk2_ref, sg_ref,
                 o_ref, *, tq, nq, lam_init):
    pair = pl.program_id(2)
    ck = 2 * tq

    def run_blocks(blocks):
        chunks = []
        for c in blocks:
            hi = (c + 1) * tq
            los = list(range(0, hi, ck))
            chunks += [(c, lo, min(lo + ck, hi), lo == los[-1]) for lo in los]
        q = {c: jnp.concatenate([q_ref[0, c * tq:(c + 1) * tq, :],
                                 q_ref[1, c * tq:(c + 1) * tq, :]], axis=0) for c in blocks}
        state = {c: None for c in blocks}

        def scores(c, lo, hi, last):
            s = jnp.dot(q[c], kt_ref[0, :, lo:hi], preferred_element_type=F32)
            if last:
                row = lax.broadcasted_iota(jnp.int32, s.shape, 0) % tq + c * tq
                col = lax.broadcasted_iota(jnp.int32, s.shape, 1) + lo
                s = jnp.where(col <= row, s, -jnp.inf)
            return s

        def fold(c, lo, hi, last, s):
            m_j = jnp.max(s, axis=-1, keepdims=True)
            if state[c] is None:
                m = m_j
                p = jnp.exp2(s - m)
                l = jnp.sum(p, axis=-1, keepdims=True)
                acc = jnp.dot(p.astype(BF16), v_ref[lo:hi, :], preferred_element_type=F32)
            else:
                m_old, l_old, acc_old = state[c]
                m = jnp.maximum(m_old, m_j)
                alpha = jnp.exp2(m_old - m)
                p = jnp.exp2(s - m)
                l = alpha * l_old + jnp.sum(p, axis=-1, keepdims=True)
                acc = alpha * acc_old + jnp.dot(p.astype(BF16), v_ref[lo:hi, :],
                                                preferred_element_type=F32)
            state[c] = (m, l, acc)
            if last:
                lam = _lambda(lq1_ref, lk1_ref, lq2_ref, lk2_ref, lam_init)
                o = acc / l
                att = o[:tq] - lam * o[tq:]
                o_ref[c * tq:(c + 1) * tq, :] = _sub_ln(att, sg_ref[...], lam_init).astype(o_ref.dtype)

        s_next = scores(*chunks[0])
        for i, ch in enumerate(chunks):
            s_cur = s_next
            if i + 1 < len(chunks):
                s_next = scores(*chunks[i + 1])
            fold(*ch, s_cur)

    for k in range(nq // 2):
        @pl.when(pair == k)
        def _pair(k=k):
            run_blocks((k, nq - 1 - k))


def _prompt_attn(q2, ktb, vb, lq1, lk1, lq2, lk2, sg, *, n_seq, seq, tq, lam_init):
    nq = seq // tq
    vec = lambda n: _const_spec((1, n))
    return pl.pallas_call(
        functools.partial(_attn_kernel, tq=tq, nq=nq, lam_init=lam_init),
        grid=(n_seq, N_HEADS, nq // 2),
        in_specs=[
            pl.BlockSpec((2, seq, HEAD_COLS), lambda b, h, i: (0, b, h)),
            pl.BlockSpec((1, HEAD_COLS, seq), lambda b, h, i: (b, h, 0)),
            pl.BlockSpec((seq, V_DIM), lambda b, h, i: (b, h)),
            vec(QK_DIM), vec(QK_DIM), vec(QK_DIM), vec(QK_DIM), vec(V_DIM),
        ],
        out_specs=pl.BlockSpec((seq, V_DIM), lambda b, h, i: (b, h)),
        out_shape=jax.ShapeDtypeStruct((n_seq * seq, ATT_WIDTH), BF16),
        compiler_params=pltpu.CompilerParams(
            dimension_semantics=("arbitrary", "arbitrary", "arbitrary"),
            vmem_limit_bytes=VMEM_LIMIT),
        name="prompt_attn",
    )(q2, ktb, vb, lq1, lk1, lq2, lk2, sg)


Q_PAD = 8


GRP = 2 * Q_PAD


def _decode_init(b, q_ref, ktn_ref, vn_ref, qbd_sc, m_sc, l_sc, acc_sc, p_sc, alpha_sc, *, n_new):
    grp = GRP
    p_sc[...] = jnp.zeros(p_sc.shape, p_sc.dtype)
    alpha_sc[...] = jnp.ones(alpha_sc.shape, alpha_sc.dtype)
    lane = lax.broadcasted_iota(jnp.int32, (Q_PAD, 512), 1)
    for h in range(N_HEADS):
        own = (lane // HEAD_COLS) == h
        for c in range(2):
            r0 = h * grp + c * Q_PAD
            qbd_sc[r0:r0 + Q_PAD, :] = jnp.where(own, q_ref[c, 0], 0.0)
    s = jnp.dot(qbd_sc[...].astype(BF16), ktn_ref[0], preferred_element_type=F32)
    t = lax.broadcasted_iota(jnp.int32, s.shape, 0) % Q_PAD
    col = lax.broadcasted_iota(jnp.int32, s.shape, 1)
    ok = ((col // n_new) == b) & ((col % n_new) <= t)
    s = jnp.where(ok, s, -jnp.inf)
    m = jnp.max(s, axis=-1, keepdims=True)
    p = jnp.exp2(s - m)
    m_sc[...] = m
    l_sc[...] = jnp.sum(p, axis=-1, keepdims=True)
    pb = p.astype(BF16)
    for h in range(N_HEADS):
        acc_sc[h * grp:(h + 1) * grp, :] = jnp.dot(
            pb[h * grp:(h + 1) * grp], vn_ref[:, h * V_DIM:(h + 1) * V_DIM],
            preferred_element_type=F32)


def _decode_scores(k_refs, qbd_sc, m_sc, l_sc, p_sc, alpha_sc):
    kt = jnp.concatenate(
        [r[...].astype(BF16) for r in k_refs], axis=-1)
    s = jnp.dot(qbd_sc[...].astype(BF16), kt, preferred_element_type=F32)
    m_old = m_sc[...]
    m_new = jnp.maximum(m_old, jnp.max(s, axis=-1, keepdims=True))
    p = jnp.exp2(s - m_new)
    alpha = jnp.exp2(m_old - m_new)
    l_sc[...] = alpha * l_sc[...] + jnp.sum(p, axis=-1, keepdims=True)
    m_sc[...] = m_new
    p_sc[...] = p.astype(BF16)
    alpha_sc[...] = alpha


def _decode_values(v_refs, p_sc, alpha_sc, acc_sc):
    grp = GRP
    pb = p_sc[...]
    alpha = alpha_sc[...]
    for hp in range(N_HEADS // 2):
        vh = jnp.concatenate(
            [jnp.concatenate(
                [r[pl.ds(2 * hp + e, PAGE, stride=N_HEADS), :].astype(BF16) for e in range(2)],
                axis=-1) for r in v_refs], axis=0)
        pv = jnp.dot(pb[2 * hp * grp:(2 * hp + 2) * grp], vh, preferred_element_type=F32)
        for e in range(2):
            sl = slice((2 * hp + e) * grp, (2 * hp + e + 1) * grp)
            acc_sc[sl, :] = alpha[sl] * acc_sc[sl, :] + pv[e * grp:(e + 1) * grp,
                                                           e * V_DIM:(e + 1) * V_DIM]


def _decode_finish(lam_refs, sg_ref, o_ref, l_sc, acc_sc, *, lam_init):
    grp = GRP
    lam = _lambda(*lam_refs, lam_init)
    o = acc_sc[...] / l_sc[...]
    for h in range(N_HEADS):
        att = o[h * grp:h * grp + Q_PAD] - lam * o[h * grp + Q_PAD:(h + 1) * grp]
        o_ref[0, :, h * V_DIM:(h + 1) * V_DIM] = _sub_ln(
            att, sg_ref[...], lam_init).astype(o_ref.dtype)


def _mix_out(x_ref, att_ref, mix_ref, wo_ref, nf_ref, o_ref):
    o_ref[...] = (x_ref[...]
                  + jnp.dot(att_ref[...], wo_ref[:ATT_WIDTH, :], preferred_element_type=F32)
                  + jnp.dot(mix_ref[...], wo_ref[ATT_WIDTH:, :], preferred_element_type=F32))
    h = o_ref[...]
    ms = jnp.mean(h * h, axis=-1, keepdims=True)
    return (h * lax.rsqrt(ms + EPS) * nf_ref[...]).astype(BF16)


def _ffn_cols(hn, wu, wd):
    up = jnp.dot(hn, wu, preferred_element_type=F32)
    return jnp.dot(jnp.square(jnp.maximum(up, 0.0)).astype(BF16), wd, preferred_element_type=F32)


FF_BLK = 1024


def _post_kernel(x_ref, att_ref, mix_ref, wo_ref, nf_ref, wu_ref, wd_ref, o_ref):
    hn = _mix_out(x_ref, att_ref, mix_ref, wo_ref, nf_ref, o_ref)
    out = o_ref[...]
    for c in range(D_FF // FF_BLK):
        cols = slice(c * FF_BLK, (c + 1) * FF_BLK)
        out = out + _ffn_cols(hn, wu_ref[:, cols], wd_ref[cols, :])
    o_ref[...] = out


def _post(x2d, att, mix, wo_b, nf, wu_b, wd_b, tm):
    rows = x2d.shape[0]
    row_spec = lambda w: pl.BlockSpec((tm, w), lambda i: (i, 0))
    return pl.pallas_call(
        _post_kernel,
        grid=(rows // tm,),
        in_specs=[
            row_spec(D_MODEL), row_spec(ATT_WIDTH), row_spec(GMLP_WIDTH),
            _const_spec((D_MODEL, D_MODEL)),
            _const_spec((1, D_MODEL)),
            _const_spec((D_MODEL, D_FF)),
            _const_spec((D_FF, D_MODEL)),
        ],
        out_specs=row_spec(D_MODEL),
        out_shape=jax.ShapeDtypeStruct((rows, D_MODEL), F32),
        compiler_params=pltpu.CompilerParams(
            dimension_semantics=("arbitrary",), vmem_limit_bytes=VMEM_LIMIT),
        name="post",
    )(x2d, att, mix, wo_b, nf, wu_b, wd_b)


N_SLOTS = 4
AHEAD = N_SLOTS - 2
SUB = 2


def _post_decode_kernel(pt_ref, x_ref, att_ref, mix_ref, wo_ref, nf_ref, wu_ref, wd_ref,
                        q_ref, ktn_ref, vn_ref, lq1_ref, lk1_ref, lq2_ref, lk2_ref, sg_ref,
                        ck_hbm, cv_hbm, o_ref, os_ref,
                        hn_sc, a_sc, qbd_sc, m_sc, l_sc, acc_sc, p_sc, alpha_sc, kbuf, vbuf, sem,
                        *, n_pages, n_new, ff_blk, lam_init):
    i = pl.program_id(0)
    j = pl.program_id(1)
    n_steps = pl.num_programs(1)
    grp_per_seq = n_steps * SUB
    n_groups = pl.num_programs(0) * grp_per_seq
    g0 = (i * n_steps + j) * SUB

    def group_copies(g, wrap=False):
        s = g % N_SLOTS
        if wrap:
            g = g % n_groups
        seq, k = g // grp_per_seq, g % grp_per_seq
        cps = []
        for p in range(n_pages):
            page = pt_ref[seq, k * n_pages + p]
            cps.append(pltpu.make_async_copy(ck_hbm.at[page], kbuf.at[s, p], sem.at[s, 0]))
            cps.append(pltpu.make_async_copy(cv_hbm.at[page], vbuf.at[s, p], sem.at[s, 1]))
        return cps

    def k_pages(g):
        return [kbuf.at[g % N_SLOTS, p] for p in range(n_pages)]

    def v_pages(g):
        return [vbuf.at[g % N_SLOTS, p] for p in range(n_pages)]

    @pl.when(g0 == 0)
    def _():
        for g in range(AHEAD):
            for cp in group_copies(g):
                cp.start()
        vbuf[N_SLOTS - 1] = jnp.zeros(vbuf.shape[1:], vbuf.dtype)

    for cp in group_copies(g0 + AHEAD, wrap=True):
        cp.start()
    for cp in group_copies(g0):
        cp.wait()

    @pl.when(j == 0)
    def _():
        hn_sc[...] = _mix_out(x_ref, att_ref, mix_ref, wo_ref, nf_ref, o_ref)
        _decode_init(i, q_ref, ktn_ref, vn_ref, qbd_sc, m_sc, l_sc, acc_sc, p_sc, alpha_sc,
                     n_new=n_new)

    c0 = pl.multiple_of(j * ff_blk, ff_blk)
    for u in range(SUB):
        g = g0 + u
        if u:
            for cp in group_copies(g + AHEAD, wrap=True):
                cp.start()
            for cp in group_copies(g):
                cp.wait()
        _decode_values(v_pages(g + N_SLOTS - 1), p_sc, alpha_sc, acc_sc)
        _decode_scores(k_pages(g), qbd_sc, m_sc, l_sc, p_sc, alpha_sc)
        if u == 0:
            up = jnp.dot(hn_sc[...], wu_ref[:, pl.ds(c0, ff_blk)], preferred_element_type=F32)
            a_sc[...] = jnp.square(jnp.maximum(up, 0.0)).astype(BF16)
        else:
            o_ref[...] += jnp.dot(a_sc[...], wd_ref[pl.ds(c0, ff_blk), :],
                                  preferred_element_type=F32)

    @pl.when(j == n_steps - 1)
    def _():
        _decode_values(v_pages(g0 + SUB - 1), p_sc, alpha_sc, acc_sc)
        _decode_finish((lq1_ref, lk1_ref, lq2_ref, lk2_ref), sg_ref, os_ref, l_sc, acc_sc,
                       lam_init=lam_init)

    @pl.when(g0 + SUB == n_groups)
    def _():
        for g in range(AHEAD):
            for cp in group_copies(n_groups + g, wrap=True):
                cp.wait()


def _post_decode(page_table, x2d, att, mix, wo_b, nf, wu_b, wd_b,
                 qpad, ktn, vn, ckt, cv, lq1, lk1, lq2, lk2, sg, *, n_new, lam_init):
    rows = x2d.shape[0]
    n_dec, n_tab = page_table.shape
    tm = rows // n_dec
    steps = 8
    ff_blk = D_FF // steps
    n_pages = n_tab // (steps * SUB)
    q_rows = N_HEADS * 2 * Q_PAD
    const = lambda shape: pl.BlockSpec(shape, lambda i, j, pt: (0,) * len(shape),
                                       pipeline_mode=pl.Buffered(1))
    row_spec = lambda w: pl.BlockSpec((tm, w), lambda i, j, pt: (i, 0))
    page_buf = pltpu.VMEM((N_SLOTS, n_pages, 512, PAGE), F32)
    grid_spec = pltpu.PrefetchScalarGridSpec(
        num_scalar_prefetch=1,
        grid=(n_dec, steps),
        in_specs=[
            row_spec(D_MODEL), row_spec(ATT_WIDTH), row_spec(GMLP_WIDTH),
            const((D_MODEL, D_MODEL)), const((1, D_MODEL)),
            const((D_MODEL, D_FF)), const((D_FF, D_MODEL)),
            pl.BlockSpec((2, 1, Q_PAD, 512), lambda i, j, pt: (0, i, 0, 0)),
            const((1, 512, n_dec * n_new)), const((n_dec * n_new, 512)),
            const((1, QK_DIM)), const((1, QK_DIM)), const((1, QK_DIM)), const((1, QK_DIM)),
            const((1, V_DIM)),
            pl.BlockSpec(memory_space=pl.ANY),
            pl.BlockSpec(memory_space=pl.ANY),
        ],
        out_specs=(row_spec(D_MODEL),
                   pl.BlockSpec((1, Q_PAD, ATT_WIDTH), lambda i, j, pt: (i, 0, 0))),
        scratch_shapes=[
            pltpu.VMEM((tm, D_MODEL), BF16),
            pltpu.VMEM((tm, ff_blk), BF16),
            pltpu.VMEM((q_rows, 512), F32),
            pltpu.VMEM((q_rows, 1), F32),
            pltpu.VMEM((q_rows, 1), F32),
            pltpu.VMEM((q_rows, V_DIM), F32),
            pltpu.VMEM((q_rows, n_pages * PAGE), BF16),
            pltpu.VMEM((q_rows, 1), F32),
            page_buf, page_buf,
            pltpu.SemaphoreType.DMA((N_SLOTS, 2)),
        ],
    )
    return pl.pallas_call(
        functools.partial(_post_decode_kernel, n_pages=n_pages, n_new=n_new, ff_blk=ff_blk,
                          lam_init=lam_init),
        grid_spec=grid_spec,
        out_shape=(jax.ShapeDtypeStruct((rows, D_MODEL), F32),
                   jax.ShapeDtypeStruct((n_dec, Q_PAD, ATT_WIDTH), F32)),
        compiler_params=pltpu.CompilerParams(
            dimension_semantics=("arbitrary", "arbitrary"), vmem_limit_bytes=VMEM_LIMIT),
        name="post_decode",
    )(page_table, x2d, att, mix, wo_b, nf, wu_b, wd_b, qpad, ktn, vn, lq1, lk1, lq2, lk2, sg,
      ckt, cv)


def kernel(x_prompt, x_sample, cache_k, cache_v, page_table, norm_mix, w_in, q_gain, k_gain,
           lambda_q1, lambda_k1, lambda_q2, lambda_k2, subln_gain, gv_gain, w_spatial,
           b_spatial, w_out, norm_ffn, w_up, w_down):
    depth = w_in.shape[0]
    assert depth == 1, "single-layer step"
    n_seq, seq, _ = x_prompt.shape
    n_dec, n_new, _ = x_sample.shape
    l = 0
    lam_init = 0.8 - 0.6 * math.exp(-0.3 * l)

    w_in_b = w_in[l].astype(BF16)
    wo_b = w_out[l].astype(BF16)
    wu_b = w_up[l].astype(BF16)
    wd_b = w_down[l].astype(BF16)
    qg = jnp.tile(q_gain[l], 2 * N_HEADS)[None]
    kg = jnp.tile(k_gain[l], 2 * N_HEADS)[None]
    gvg = gv_gain[l].reshape(1, GMLP_WIDTH)
    nm = norm_mix[l][None]
    nf = norm_ffn[l][None]
    sg = subln_gain[l][None]
    lq1, lk1, lq2, lk2 = (a[l][None] for a in (lambda_q1, lambda_k1, lambda_q2, lambda_k2))

    def gating_params(chunk):
        reps = GMLP_CHUNK // chunk
        bs_t = jnp.tile(b_spatial[l][:, :chunk], (1, reps)).T
        return w_spatial[l], bs_t

    xp = x_prompt.reshape(n_seq * seq, D_MODEL)
    q2, kt, ktb, v, vb, mix, _ = _proj(
        xp, seq, 512, GMLP_CHUNK, nm, w_in_b, qg, kg, gvg, *gating_params(GMLP_CHUNK))
    att = _prompt_attn(q2, ktb, vb, lq1, lk1, lq2, lk2, sg,
                       n_seq=n_seq, seq=seq, tq=256, lam_init=lam_init)
    k_prompt = kt.reshape(n_seq, N_HEADS, 2, QK_DIM, seq).transpose(0, 4, 1, 2, 3)[None]
    v_prompt = v.reshape(1, n_seq, seq, N_HEADS, V_DIM)

    rows_s = n_dec * n_new
    xs = x_sample.reshape(rows_s, D_MODEL)
    q2s, kts, ktbs, vs, vbs, mixs, gvs = _proj(
        xs, rows_s, rows_s, n_new, nm, w_in_b, qg, kg, gvg, *gating_params(n_new))
    qpad = jnp.pad(q2s.reshape(2, n_dec, n_new, 512).astype(F32),
                   ((0, 0), (0, 0), (0, Q_PAD - n_new), (0, 0)))
    n_phys = cache_k.shape[1]
    ckt = cache_k[l].transpose(0, 2, 3, 4, 1).reshape(n_phys, 512, PAGE)
    cv = cache_v[l].reshape(n_phys, PAGE * N_HEADS, V_DIM)
    y_prompt, att_s = _post_decode(page_table, xp, att, mix, wo_b, nf, wu_b, wd_b,
                                   qpad, ktbs, vbs, ckt, cv, lq1, lk1, lq2, lk2, sg,
                                   n_new=n_new, lam_init=lam_init)
    y_prompt = y_prompt.reshape(n_seq, seq, D_MODEL)
    att_s = att_s[:, :n_new].reshape(rows_s, ATT_WIDTH).astype(BF16)
    y_sample = _post(xs, att_s, mixs, wo_b, nf, wu_b, wd_b, rows_s).reshape(n_dec, n_new, D_MODEL)
    k_sample = kts[0].T.reshape(1, n_dec, n_new, N_HEADS, 2, QK_DIM)
    v_sample = vs.reshape(1, n_dec, n_new, N_HEADS, V_DIM)
    gv_sample = gvs.reshape(1, n_dec, n_new, GMLP_WIDTH)

    return (y_prompt, y_sample, k_prompt, v_prompt, k_sample, v_sample, gv_sample)
```

```python
import functools
import math

import jax
import jax.numpy as jnp
from jax import lax
from jax.experimental import pallas as pl
from jax.experimental.pallas import tpu as pltpu

F32 = jnp.float32
BF16 = jnp.bfloat16

D_MODEL = 1024
N_HEADS = 4
QK_DIM = 64
V_DIM = 128
HEAD_COLS = 2 * QK_DIM
ATT_WIDTH = N_HEADS * V_DIM
GMLP_WIDTH = 512
N_GROUPS = 4
GROUP_CH = 128
GMLP_CHUNK = 128
D_FF = 4096
PAGE = 128
EPS = 1e-6
LANES = 128
VMEM_LIMIT = 56 * 1024 * 1024

GELU_C = math.sqrt(2.0 / math.pi)
LOG2E = math.log2(math.e)


def _gelu(x):
    return 0.5 * x * (1.0 + jnp.tanh(GELU_C * (x + 0.044715 * (x * x * x))))


def _const_spec(shape):
    nd = len(shape)
    return pl.BlockSpec(shape, lambda *_: (0,) * nd, pipeline_mode=pl.Buffered(1))


def _half_group_rms(t):
    lane = lax.broadcasted_iota(jnp.int32, (1, LANES), 1)
    lo = lane < QK_DIM
    outs = []
    for j in range(t.shape[1] // LANES):
        c = t[:, j * LANES:(j + 1) * LANES]
        c2 = c * c
        s_lo = jnp.sum(jnp.where(lo, c2, 0.0), axis=-1, keepdims=True)
        s_hi = jnp.sum(jnp.where(lo, 0.0, c2), axis=-1, keepdims=True)
        r = jnp.where(lo, lax.rsqrt(s_lo * (1.0 / QK_DIM) + EPS),
                      lax.rsqrt(s_hi * (1.0 / QK_DIM) + EPS))
        outs.append(c * r)
    return jnp.concatenate(outs, axis=-1)


def _proj_kernel(x_ref, nm_ref, w_ref, qg_ref, kg_ref, gvg_ref, ws_ref, bs_ref,
                 q_ref, kt_ref, ktb_ref, v_ref, vb_ref, mix_ref, gv_ref, *, chunk):
    tm = x_ref.shape[0]
    x = x_ref[...]
    ms = jnp.mean(x * x, axis=-1, keepdims=True)
    xn = (x * lax.rsqrt(ms + EPS) * nm_ref[...]).astype(BF16)

    def seg(i):
        return jnp.dot(xn, w_ref[:, i * 512:(i + 1) * 512], preferred_element_type=F32)

    lane = lax.broadcasted_iota(jnp.int32, (1, 512), 1)
    first_comp = (lane % HEAD_COLS) < QK_DIM

    z_u = seg(3)
    z_g = seg(4)
    u = _gelu(z_u)
    z_q = seg(0)
    gr = _gelu(z_g)
    z_k = seg(1)

    qn = _half_group_rms(z_q) * (qg_ref[...] * (QK_DIM ** -0.5 * LOG2E))
    q_ref[0] = jnp.where(first_comp, qn, 0.0).astype(BF16)
    q_ref[1] = jnp.where(first_comp, 0.0, qn).astype(BF16)
    v = seg(2)

    kn = _half_group_rms(z_k) * kg_ref[...]
    knt = kn.T
    kt_ref[0] = knt
    ktb_ref[0] = knt.astype(BF16)

    for h in range(N_HEADS):
        v_ref[pl.ds(h, tm, stride=N_HEADS), :] = v[:, h * V_DIM:(h + 1) * V_DIM]
    vb_ref[...] = v.astype(BF16)

    gvs = []
    for g in range(N_GROUPS):
        c = gr[:, g * GROUP_CH:(g + 1) * GROUP_CH]
        r = lax.rsqrt(jnp.mean(c * c, axis=-1, keepdims=True) + EPS)
        gvs.append(c * r * gvg_ref[:, g * GROUP_CH:(g + 1) * GROUP_CH])
    gv_ref[...] = jnp.concatenate(gvs, axis=-1)

    row = lax.broadcasted_iota(jnp.int32, (GMLP_CHUNK, GMLP_CHUNK), 0)
    col = lax.broadcasted_iota(jnp.int32, (GMLP_CHUNK, GMLP_CHUNK), 1)
    keep = (col <= row) & ((row // chunk) == (col // chunk))
    sel = jnp.where(col == row % chunk, 1.0, 0.0).astype(BF16)
    n_blk = tm // GMLP_CHUNK
    for g in range(N_GROUPS):
        wg = ws_ref[g].astype(BF16)
        if chunk < GMLP_CHUNK:
            wg = jnp.dot(sel, wg, preferred_element_type=F32).astype(BF16)
            wg = lax.dot_general(wg, sel, (((1,), (1,)), ((), ())),
                                 preferred_element_type=F32).astype(BF16)
        wm = jnp.where(keep, wg, jnp.zeros_like(wg))
        rhs = jnp.concatenate(
            [gvs[g][b * GMLP_CHUNK:(b + 1) * GMLP_CHUNK].astype(BF16) for b in range(n_blk)],
            axis=-1)
        s = jnp.dot(wm, rhs, preferred_element_type=F32)
        bias = bs_ref[:, g:g + 1]
        for b in range(n_blk):
            sb = s[:, b * GROUP_CH:(b + 1) * GROUP_CH] + bias
            ub = u[b * GMLP_CHUNK:(b + 1) * GMLP_CHUNK, g * GROUP_CH:(g + 1) * GROUP_CH]
            mix_ref[b * GMLP_CHUNK:(b + 1) * GMLP_CHUNK,
                    g * GROUP_CH:(g + 1) * GROUP_CH] = (ub * sb).astype(BF16)


def _proj(x2d, seq, tm, chunk, nm, w_in_b, qg, kg, gvg, ws_t, bs_t):
    rows = x2d.shape[0]
    n_seq = rows // seq
    per_seq = seq // tm
    grid = (rows // tm,)
    row_spec = lambda w: pl.BlockSpec((tm, w), lambda i: (i, 0))
    out_shape = (
        jax.ShapeDtypeStruct((2, rows, 512), BF16),
        jax.ShapeDtypeStruct((n_seq, 512, seq), F32),
        jax.ShapeDtypeStruct((n_seq, 512, seq), BF16),
        jax.ShapeDtypeStruct((rows * N_HEADS, V_DIM), F32),
        jax.ShapeDtypeStruct((rows, 512), BF16),
        jax.ShapeDtypeStruct((rows, 512), BF16),
        jax.ShapeDtypeStruct((rows, 512), F32),
    )
    kt_spec = pl.BlockSpec((1, 512, tm), lambda i: (i // per_seq, 0, i % per_seq))
    return pl.pallas_call(
        functools.partial(_proj_kernel, chunk=chunk),
        grid=grid,
        in_specs=[
            row_spec(D_MODEL),
            _const_spec((1, D_MODEL)),
            _const_spec((D_MODEL, 2560)),
            _const_spec((1, 512)),
            _const_spec((1, 512)),
            _const_spec((1, 512)),
            _const_spec((N_GROUPS, GMLP_CHUNK, GMLP_CHUNK)),
            _const_spec((GMLP_CHUNK, N_GROUPS)),
        ],
        out_specs=(
            pl.BlockSpec((2, tm, 512), lambda i: (0, i, 0)),
            kt_spec, kt_spec,
            pl.BlockSpec((tm * N_HEADS, V_DIM), lambda i: (i, 0)),
            row_spec(512), row_spec(512), row_spec(512),
        ),
        out_shape=out_shape,
        compiler_params=pltpu.CompilerParams(
            dimension_semantics=("arbitrary",), vmem_limit_bytes=VMEM_LIMIT),
        name="proj",
    )(x2d, nm, w_in_b, qg, kg, gvg, ws_t, bs_t)


def _lambda(lq1_ref, lk1_ref, lq2_ref, lk2_ref, lam_init):
    a = jnp.sum(lq1_ref[...] * lk1_ref[...], axis=-1, keepdims=True)
    b = jnp.sum(lq2_ref[...] * lk2_ref[...], axis=-1, keepdims=True)
    return jnp.exp(a) - jnp.exp(b) + lam_init


def _sub_ln(att, gain, lam_init):
    r = lax.rsqrt(jnp.mean(att * att, axis=-1, keepdims=True) + EPS)
    return att * r * gain * (1.0 - lam_init)


def _attn_kernel(q_ref, kt_ref, v_ref, lq1_ref, lk1_ref, lq2_ref, lk2_ref, sg_ref,
                 o_ref, *, tq, nq, lam_init):
    ck = 2 * tq

    def run_blocks(blocks):
        chunks = []
        for c in blocks:
            hi = (c + 1) * tq
            los = list(range(0, hi, ck))
            chunks += [(c, lo, min(lo + ck, hi), lo == los[-1]) for lo in los]
        q = {c: jnp.concatenate([q_ref[0, c * tq:(c + 1) * tq, :],
                                 q_ref[1, c * tq:(c + 1) * tq, :]], axis=0) for c in blocks}
        state = {c: None for c in blocks}

        def scores(c, lo, hi, last):
            s = jnp.dot(q[c], kt_ref[0, :, lo:hi], preferred_element_type=F32)
            if last:
                row = lax.broadcasted_iota(jnp.int32, s.shape, 0) % tq + c * tq
                col = lax.broadcasted_iota(jnp.int32, s.shape, 1) + lo
                s = jnp.where(col <= row, s, -jnp.inf)
            return s

        def fold(c, lo, hi, last, s):
            m_j = jnp.max(s, axis=-1, keepdims=True)
            if state[c] is None:
                m = m_j
                p = jnp.exp2(s - m)
                l = jnp.sum(p, axis=-1, keepdims=True)
                acc = jnp.dot(p.astype(BF16), v_ref[lo:hi, :], preferred_element_type=F32)
            else:
                m_old, l_old, acc_old = state[c]
                m = jnp.maximum(m_old, m_j)
                alpha = jnp.exp2(m_old - m)
                p = jnp.exp2(s - m)
                l = alpha * l_old + jnp.sum(p, axis=-1, keepdims=True)
                acc = alpha * acc_old + jnp.dot(p.astype(BF16), v_ref[lo:hi, :],
                                                preferred_element_type=F32)
            state[c] = (m, l, acc)
            if last:
                lam = _lambda(lq1_ref, lk1_ref, lq2_ref, lk2_ref, lam_init)
                o = acc / l
                att = o[:tq] - lam * o[tq:]
                o_ref[c * tq:(c + 1) * tq, :] = _sub_ln(att, sg_ref[...], lam_init).astype(o_ref.dtype)

        s_next = scores(*chunks[0])
        for i, ch in enumerate(chunks):
            s_cur = s_next
            if i + 1 < len(chunks):
                s_next = scores(*chunks[i + 1])
            fold(*ch, s_cur)

    run_blocks(tuple(range(nq)))


def _prompt_attn(q2, ktb, vb, lq1, lk1, lq2, lk2, sg, *, n_seq, seq, tq, lam_init):
    nq = seq // tq
    vec = lambda n: _const_spec((1, n))
    return pl.pallas_call(
        functools.partial(_attn_kernel, tq=tq, nq=nq, lam_init=lam_init),
        grid=(n_seq, N_HEADS),
        in_specs=[
            pl.BlockSpec((2, seq, HEAD_COLS), lambda b, h: (0, b, h)),
            pl.BlockSpec((1, HEAD_COLS, seq), lambda b, h: (b, h, 0)),
            pl.BlockSpec((seq, V_DIM), lambda b, h: (b, h)),
            vec(QK_DIM), vec(QK_DIM), vec(QK_DIM), vec(QK_DIM), vec(V_DIM),
        ],
        out_specs=pl.BlockSpec((seq, V_DIM), lambda b, h: (b, h)),
        out_shape=jax.ShapeDtypeStruct((n_seq * seq, ATT_WIDTH), BF16),
        compiler_params=pltpu.CompilerParams(
            dimension_semantics=("arbitrary", "arbitrary"),
            vmem_limit_bytes=VMEM_LIMIT),
        name="prompt_attn",
    )(q2, ktb, vb, lq1, lk1, lq2, lk2, sg)


Q_PAD = 8


GRP = 2 * Q_PAD


def _decode_init(b, q_ref, ktn_ref, vn_ref, qbd_sc, m_sc, l_sc, acc_sc, p_sc, alpha_sc, *, n_new):
    grp = GRP
    p_sc[...] = jnp.zeros(p_sc.shape, p_sc.dtype)
    alpha_sc[...] = jnp.ones(alpha_sc.shape, alpha_sc.dtype)
    lane = lax.broadcasted_iota(jnp.int32, (Q_PAD, 512), 1)
    for h in range(N_HEADS):
        own = (lane // HEAD_COLS) == h
        for c in range(2):
            r0 = h * grp + c * Q_PAD
            qbd_sc[r0:r0 + Q_PAD, :] = jnp.where(own, q_ref[c, 0], 0.0)
    s = jnp.dot(qbd_sc[...].astype(BF16), ktn_ref[0], preferred_element_type=F32)
    t = lax.broadcasted_iota(jnp.int32, s.shape, 0) % Q_PAD
    col = lax.broadcasted_iota(jnp.int32, s.shape, 1)
    ok = ((col // n_new) == b) & ((col % n_new) <= t)
    s = jnp.where(ok, s, -jnp.inf)
    m = jnp.max(s, axis=-1, keepdims=True)
    p = jnp.exp2(s - m)
    m_sc[...] = m
    l_sc[...] = jnp.sum(p, axis=-1, keepdims=True)
    pb = p.astype(BF16)
    for h in range(N_HEADS):
        acc_sc[h * grp:(h + 1) * grp, :] = jnp.dot(
            pb[h * grp:(h + 1) * grp], vn_ref[:, h * V_DIM:(h + 1) * V_DIM],
            preferred_element_type=F32)


def _decode_scores(k_refs, qbd_sc, m_sc, l_sc, p_sc, alpha_sc):
    kt = jnp.concatenate(
        [r[...].astype(BF16) for r in k_refs], axis=-1)
    s = jnp.dot(qbd_sc[...].astype(BF16), kt, preferred_element_type=F32)
    m_old = m_sc[...]
    m_new = jnp.maximum(m_old, jnp.max(s, axis=-1, keepdims=True))
    p = jnp.exp2(s - m_new)
    alpha = jnp.exp2(m_old - m_new)
    l_sc[...] = alpha * l_sc[...] + jnp.sum(p, axis=-1, keepdims=True)
    m_sc[...] = m_new
    p_sc[...] = p.astype(BF16)
    alpha_sc[...] = alpha


def _decode_values(v_refs, p_sc, alpha_sc, acc_sc):
    grp = GRP
    pb = p_sc[...]
    alpha = alpha_sc[...]
    for hp in range(N_HEADS // 2):
        vh = jnp.concatenate(
            [jnp.concatenate(
                [r[pl.ds(2 * hp + e, PAGE, stride=N_HEADS), :].astype(BF16) for e in range(2)],
                axis=-1) for r in v_refs], axis=0)
        pv = jnp.dot(pb[2 * hp * grp:(2 * hp + 2) * grp], vh, preferred_element_type=F32)
        for e in range(2):
            sl = slice((2 * hp + e) * grp, (2 * hp + e + 1) * grp)
            acc_sc[sl, :] = alpha[sl] * acc_sc[sl, :] + pv[e * grp:(e + 1) * grp,
                                                           e * V_DIM:(e + 1) * V_DIM]


def _decode_finish(lam_refs, sg_ref, o_ref, l_sc, acc_sc, *, lam_init):
    grp = GRP
    lam = _lambda(*lam_refs, lam_init)
    o = acc_sc[...] / l_sc[...]
    for h in range(N_HEADS):
        att = o[h * grp:h * grp + Q_PAD] - lam * o[h * grp + Q_PAD:(h + 1) * grp]
        o_ref[0, :, h * V_DIM:(h + 1) * V_DIM] = _sub_ln(
            att, sg_ref[...], lam_init).astype(o_ref.dtype)


def _mix_out(x_ref, att_ref, mix_ref, wo_ref, nf_ref, o_ref):
    o_ref[...] = (x_ref[...]
                  + jnp.dot(att_ref[...], wo_ref[:ATT_WIDTH, :], preferred_element_type=F32)
                  + jnp.dot(mix_ref[...], wo_ref[ATT_WIDTH:, :], preferred_element_type=F32))
    h = o_ref[...]
    ms = jnp.mean(h * h, axis=-1, keepdims=True)
    return (h * lax.rsqrt(ms + EPS) * nf_ref[...]).astype(BF16)


def _ffn_cols(hn, wu, wd):
    up = jnp.dot(hn, wu, preferred_element_type=F32)
    return jnp.dot(jnp.square(jnp.maximum(up, 0.0)).astype(BF16), wd, preferred_element_type=F32)


FF_BLK = 1024


def _post_kernel(x_ref, att_ref, mix_ref, wo_ref, nf_ref, wu_ref, wd_ref, o_ref):
    hn = _mix_out(x_ref, att_ref, mix_ref, wo_ref, nf_ref, o_ref)
    out = o_ref[...]
    for c in range(D_FF // FF_BLK):
        cols = slice(c * FF_BLK, (c + 1) * FF_BLK)
        out = out + _ffn_cols(hn, wu_ref[:, cols], wd_ref[cols, :])
    o_ref[...] = out


def _post(x2d, att, mix, wo_b, nf, wu_b, wd_b, tm):
    rows = x2d.shape[0]
    row_spec = lambda w: pl.BlockSpec((tm, w), lambda i: (i, 0))
    return pl.pallas_call(
        _post_kernel,
        grid=(rows // tm,),
        in_specs=[
            row_spec(D_MODEL), row_spec(ATT_WIDTH), row_spec(GMLP_WIDTH),
            _const_spec((D_MODEL, D_MODEL)),
            _const_spec((1, D_MODEL)),
            _const_spec((D_MODEL, D_FF)),
            _const_spec((D_FF, D_MODEL)),
        ],
        out_specs=row_spec(D_MODEL),
        out_shape=jax.ShapeDtypeStruct((rows, D_MODEL), F32),
        compiler_params=pltpu.CompilerParams(
            dimension_semantics=("arbitrary",), vmem_limit_bytes=VMEM_LIMIT),
        name="post",
    )(x2d, att, mix, wo_b, nf, wu_b, wd_b)


N_SLOTS = 4
AHEAD = N_SLOTS - 2
SUB = 2


def _post_decode_kernel(pt_ref, x_ref, att_ref, mix_ref, wo_ref, nf_ref, wu_ref, wd_ref,
                        q_ref, ktn_ref, vn_ref, lq1_ref, lk1_ref, lq2_ref, lk2_ref, sg_ref,
                        ck_hbm, cv_hbm, o_ref, os_ref,
                        hn_sc, a_sc, qbd_sc, m_sc, l_sc, acc_sc, p_sc, alpha_sc, kbuf, vbuf, sem,
                        *, n_pages, n_new, ff_blk, lam_init):
    i = pl.program_id(0)
    j = pl.program_id(1)
    n_steps = pl.num_programs(1)
    grp_per_seq = n_steps * SUB
    n_groups = pl.num_programs(0) * grp_per_seq
    g0 = (i * n_steps + j) * SUB

    def group_copies(g, wrap=False):
        s = g % N_SLOTS
        if wrap:
            g = g % n_groups
        seq, k = g // grp_per_seq, g % grp_per_seq
        cps = []
        for p in range(n_pages):
            page = pt_ref[seq, k * n_pages + p]
            cps.append(pltpu.make_async_copy(ck_hbm.at[page], kbuf.at[s, p], sem.at[s, 0]))
            cps.append(pltpu.make_async_copy(cv_hbm.at[page], vbuf.at[s, p], sem.at[s, 1]))
        return cps

    def k_pages(g):
        return [kbuf.at[g % N_SLOTS, p] for p in range(n_pages)]

    def v_pages(g):
        return [vbuf.at[g % N_SLOTS, p] for p in range(n_pages)]

    @pl.when(g0 == 0)
    def _():
        for g in range(AHEAD):
            for cp in group_copies(g):
                cp.start()
        vbuf[N_SLOTS - 1] = jnp.zeros(vbuf.shape[1:], vbuf.dtype)

    for cp in group_copies(g0 + AHEAD, wrap=True):
        cp.start()
    for cp in group_copies(g0):
        cp.wait()

    @pl.when(j == 0)
    def _():
        hn_sc[...] = _mix_out(x_ref, att_ref, mix_ref, wo_ref, nf_ref, o_ref)
        _decode_init(i, q_ref, ktn_ref, vn_ref, qbd_sc, m_sc, l_sc, acc_sc, p_sc, alpha_sc,
                     n_new=n_new)

    c0 = pl.multiple_of(j * ff_blk, ff_blk)
    for u in range(SUB):
        g = g0 + u
        if u:
            for cp in group_copies(g + AHEAD, wrap=True):
                cp.start()
            for cp in group_copies(g):
                cp.wait()
        _decode_values(v_pages(g + N_SLOTS - 1), p_sc, alpha_sc, acc_sc)
        _decode_scores(k_pages(g), qbd_sc, m_sc, l_sc, p_sc, alpha_sc)
        if u == 0:
            up = jnp.dot(hn_sc[...], wu_ref[:, pl.ds(c0, ff_blk)], preferred_element_type=F32)
            a_sc[...] = jnp.square(jnp.maximum(up, 0.0)).astype(BF16)
        else:
            o_ref[...] += jnp.dot(a_sc[...], wd_ref[pl.ds(c0, ff_blk), :],
                                  preferred_element_type=F32)

    @pl.when(j == n_steps - 1)
    def _():
        _decode_values(v_pages(g0 + SUB - 1), p_sc, alpha_sc, acc_sc)
        _decode_finish((lq1_ref, lk1_ref, lq2_ref, lk2_ref), sg_ref, os_ref, l_sc, acc_sc,
                       lam_init=lam_init)

    @pl.when(g0 + SUB == n_groups)
    def _():
        for g in range(AHEAD):
            for cp in group_copies(n_groups + g, wrap=True):
                cp.wait()


def _post_decode(page_table, x2d, att, mix, wo_b, nf, wu_b, wd_b,
                 qpad, ktn, vn, ckt, cv, lq1, lk1, lq2, lk2, sg, *, n_new, lam_init):
    rows = x2d.shape[0]
    n_dec, n_tab = page_table.shape
    tm = rows // n_dec
    steps = 8
    ff_blk = D_FF // steps
    n_pages = n_tab // (steps * SUB)
    q_rows = N_HEADS * 2 * Q_PAD
    const = lambda shape: pl.BlockSpec(shape, lambda i, j, pt: (0,) * len(shape),
                                       pipeline_mode=pl.Buffered(1))
    row_spec = lambda w: pl.BlockSpec((tm, w), lambda i, j, pt: (i, 0))
    page_buf = pltpu.VMEM((N_SLOTS, n_pages, 512, PAGE), F32)
    grid_spec = pltpu.PrefetchScalarGridSpec(
        num_scalar_prefetch=1,
        grid=(n_dec, steps),
        in_specs=[
            row_spec(D_MODEL), row_spec(ATT_WIDTH), row_spec(GMLP_WIDTH),
            const((D_MODEL, D_MODEL)), const((1, D_MODEL)),
            const((D_MODEL, D_FF)), const((D_FF, D_MODEL)),
            pl.BlockSpec((2, 1, Q_PAD, 512), lambda i, j, pt: (0, i, 0, 0)),
            const((1, 512, n_dec * n_new)), const((n_dec * n_new, 512)),
            const((1, QK_DIM)), const((1, QK_DIM)), const((1, QK_DIM)), const((1, QK_DIM)),
            const((1, V_DIM)),
            pl.BlockSpec(memory_space=pl.ANY),
            pl.BlockSpec(memory_space=pl.ANY),
        ],
        out_specs=(row_spec(D_MODEL),
                   pl.BlockSpec((1, Q_PAD, ATT_WIDTH), lambda i, j, pt: (i, 0, 0))),
        scratch_shapes=[
            pltpu.VMEM((tm, D_MODEL), BF16),
            pltpu.VMEM((tm, ff_blk), BF16),
            pltpu.VMEM((q_rows, 512), F32),
            pltpu.VMEM((q_rows, 1), F32),
            pltpu.VMEM((q_rows, 1), F32),
            pltpu.VMEM((q_rows, V_DIM), F32),
            pltpu.VMEM((q_rows, n_pages * PAGE), BF16),
            pltpu.VMEM((q_rows, 1), F32),
            page_buf, page_buf,
            pltpu.SemaphoreType.DMA((N_SLOTS, 2)),
        ],
    )
    return pl.pallas_call(
        functools.partial(_post_decode_kernel, n_pages=n_pages, n_new=n_new, ff_blk=ff_blk,
                          lam_init=lam_init),
        grid_spec=grid_spec,
        out_shape=(jax.ShapeDtypeStruct((rows, D_MODEL), F32),
                   jax.ShapeDtypeStruct((n_dec, Q_PAD, ATT_WIDTH), F32)),
        compiler_params=pltpu.CompilerParams(
            dimension_semantics=("arbitrary", "arbitrary"), vmem_limit_bytes=VMEM_LIMIT),
        name="post_decode",
    )(page_table, x2d, att, mix, wo_b, nf, wu_b, wd_b, qpad, ktn, vn, lq1, lk1, lq2, lk2, sg,
      ckt, cv)


def kernel(x_prompt, x_sample, cache_k, cache_v, page_table, norm_mix, w_in, q_gain, k_gain,
           lambda_q1, lambda_k1, lambda_q2, lambda_k2, subln_gain, gv_gain, w_spatial,
           b_spatial, w_out, norm_ffn, w_up, w_down):
    depth = w_in.shape[0]
    assert depth == 1, "single-layer step"
    n_seq, seq, _ = x_prompt.shape
    n_dec, n_new, _ = x_sample.shape
    l = 0
    lam_init = 0.8 - 0.6 * math.exp(-0.3 * l)

    w_in_b = w_in[l].astype(BF16)
    wo_b = w_out[l].astype(BF16)
    wu_b = w_up[l].astype(BF16)
    wd_b = w_down[l].astype(BF16)
    qg = jnp.tile(q_gain[l], 2 * N_HEADS)[None]
    kg = jnp.tile(k_gain[l], 2 * N_HEADS)[None]
    gvg = gv_gain[l].reshape(1, GMLP_WIDTH)
    nm = norm_mix[l][None]
    nf = norm_ffn[l][None]
    sg = subln_gain[l][None]
    lq1, lk1, lq2, lk2 = (a[l][None] for a in (lambda_q1, lambda_k1, lambda_q2, lambda_k2))

    def gating_params(chunk):
        reps = GMLP_CHUNK // chunk
        bs_t = jnp.tile(b_spatial[l][:, :chunk], (1, reps)).T
        return w_spatial[l], bs_t

    xp = x_prompt.reshape(n_seq * seq, D_MODEL)
    q2, kt, ktb, v, vb, mix, _ = _proj(
        xp, seq, 512, GMLP_CHUNK, nm, w_in_b, qg, kg, gvg, *gating_params(GMLP_CHUNK))
    att = _prompt_attn(q2, ktb, vb, lq1, lk1, lq2, lk2, sg,
                       n_seq=n_seq, seq=seq, tq=256, lam_init=lam_init)
    k_prompt = kt.reshape(n_seq, N_HEADS, 2, QK_DIM, seq).transpose(0, 4, 1, 2, 3)[None]
    v_prompt = v.reshape(1, n_seq, seq, N_HEADS, V_DIM)

    rows_s = n_dec * n_new
    xs = x_sample.reshape(rows_s, D_MODEL)
    q2s, kts, ktbs, vs, vbs, mixs, gvs = _proj(
        xs, rows_s, rows_s, n_new, nm, w_in_b, qg, kg, gvg, *gating_params(n_new))
    qpad = jnp.pad(q2s.reshape(2, n_dec, n_new, 512).astype(F32),
                   ((0, 0), (0, 0), (0, Q_PAD - n_new), (0, 0)))
    n_phys = cache_k.shape[1]
    ckt = cache_k[l].transpose(0, 2, 3, 4, 1).reshape(n_phys, 512, PAGE)
    cv = cache_v[l].reshape(n_phys, PAGE * N_HEADS, V_DIM)
    y_prompt, att_s = _post_decode(page_table, xp, att, mix, wo_b, nf, wu_b, wd_b,
                                   qpad, ktbs, vbs, ckt, cv, lq1, lk1, lq2, lk2, sg,
                                   n_new=n_new, lam_init=lam_init)
    y_prompt = y_prompt.reshape(n_seq, seq, D_MODEL)
    att_s = att_s[:, :n_new].reshape(rows_s, ATT_WIDTH).astype(BF16)
    y_sample = _post(xs, att_s, mixs, wo_b, nf, wu_b, wd_b, rows_s).reshape(n_dec, n_new, D_MODEL)
    k_sample = kts[0].T.reshape(1, n_dec, n_new, N_HEADS, 2, QK_DIM)
    v_sample = vs.reshape(1, n_dec, n_new, N_HEADS, V_DIM)
    gv_sample = gvs.reshape(1, n_dec, n_new, GMLP_WIDTH)

    return (y_prompt, y_sample, k_prompt, v_prompt, k_sample, v_sample, gv_sample)
```

```python
import functools
import math

import jax
import jax.numpy as jnp
from jax import lax
from jax.experimental import pallas as pl
from jax.experimental.pallas import tpu as pltpu

F32 = jnp.float32
BF16 = jnp.bfloat16

D_MODEL = 1024
N_HEADS = 4
QK_DIM = 64
V_DIM = 128
HEAD_COLS = 2 * QK_DIM
ATT_WIDTH = N_HEADS * V_DIM
GMLP_WIDTH = 512
N_GROUPS = 4
GROUP_CH = 128
GMLP_CHUNK = 128
D_FF = 4096
PAGE = 128
EPS = 1e-6
LANES = 128
VMEM_LIMIT = 56 * 1024 * 1024

GELU_C = math.sqrt(2.0 / math.pi)
LOG2E = math.log2(math.e)


def _gelu(x):
    hx = 0.5 * x
    return hx + hx * jnp.tanh(x * (GELU_C + (GELU_C * 0.044715) * (x * x)))


def _const_spec(shape):
    nd = len(shape)
    return pl.BlockSpec(shape, lambda *_: (0,) * nd, pipeline_mode=pl.Buffered(1))


def _half_group_rms(t):
    lane = lax.broadcasted_iota(jnp.int32, (1, LANES), 1)
    lo = lane < QK_DIM
    outs = []
    for j in range(t.shape[1] // LANES):
        c = t[:, j * LANES:(j + 1) * LANES]
        c2 = c * c
        s_lo = jnp.sum(jnp.where(lo, c2, 0.0), axis=-1, keepdims=True)
        s_hi = jnp.sum(jnp.where(lo, 0.0, c2), axis=-1, keepdims=True)
        r = jnp.where(lo, lax.rsqrt(s_lo * (1.0 / QK_DIM) + EPS),
                      lax.rsqrt(s_hi * (1.0 / QK_DIM) + EPS))
        outs.append(c * r)
    return jnp.concatenate(outs, axis=-1)


def _proj_kernel(x_ref, nm_ref, w_ref, qg_ref, kg_ref, gvg_ref, ws_ref, bs_ref,
                 q_ref, kt_ref, ktb_ref, v_ref, vb_ref, mix_ref, gv_ref, *, chunk):
    tm = x_ref.shape[0]
    x = x_ref[...]
    ms = jnp.mean(x * x, axis=-1, keepdims=True)
    xn = (x * lax.rsqrt(ms + EPS) * nm_ref[...]).astype(BF16)

    def seg(i):
        return jnp.dot(xn, w_ref[:, i * 512:(i + 1) * 512], preferred_element_type=F32)

    lane = lax.broadcasted_iota(jnp.int32, (1, 512), 1)
    first_comp = (lane % HEAD_COLS) < QK_DIM

    z_u = seg(3)
    z_g = seg(4)
    u = _gelu(z_u)
    z_q = seg(0)
    gr = _gelu(z_g)
    z_k = seg(1)

    qn = _half_group_rms(z_q) * (qg_ref[...] * (QK_DIM ** -0.5 * LOG2E))
    q_ref[0] = jnp.where(first_comp, qn, 0.0).astype(BF16)
    q_ref[1] = jnp.where(first_comp, 0.0, qn).astype(BF16)
    v = seg(2)

    kn = _half_group_rms(z_k) * kg_ref[...]
    knt = kn.T
    kt_ref[0] = knt
    ktb_ref[0] = knt.astype(BF16)

    for h in range(N_HEADS):
        v_ref[pl.ds(h, tm, stride=N_HEADS), :] = v[:, h * V_DIM:(h + 1) * V_DIM]
    vb_ref[...] = v.astype(BF16)

    gvs = []
    for g in range(N_GROUPS):
        c = gr[:, g * GROUP_CH:(g + 1) * GROUP_CH]
        r = lax.rsqrt(jnp.mean(c * c, axis=-1, keepdims=True) + EPS)
        gvs.append(c * r * gvg_ref[:, g * GROUP_CH:(g + 1) * GROUP_CH])
    gv_ref[...] = jnp.concatenate(gvs, axis=-1)

    row = lax.broadcasted_iota(jnp.int32, (GMLP_CHUNK, GMLP_CHUNK), 0)
    col = lax.broadcasted_iota(jnp.int32, (GMLP_CHUNK, GMLP_CHUNK), 1)
    keep = (col <= row) & ((row // chunk) == (col // chunk))
    sel = jnp.where(col == row % chunk, 1.0, 0.0).astype(BF16)
    n_blk = tm // GMLP_CHUNK
    for g in range(N_GROUPS):
        wg = ws_ref[g].astype(BF16)
        if chunk < GMLP_CHUNK:
            wg = jnp.dot(sel, wg, preferred_element_type=F32).astype(BF16)
            wg = lax.dot_general(wg, sel, (((1,), (1,)), ((), ())),
                                 preferred_element_type=F32).astype(BF16)
        wm = jnp.where(keep, wg, jnp.zeros_like(wg))
        rhs = jnp.concatenate(
            [gvs[g][b * GMLP_CHUNK:(b + 1) * GMLP_CHUNK].astype(BF16) for b in range(n_blk)],
            axis=-1)
        s = jnp.dot(wm, rhs, preferred_element_type=F32)
        bias = bs_ref[:, g:g + 1]
        for b in range(n_blk):
            sb = s[:, b * GROUP_CH:(b + 1) * GROUP_CH] + bias
            ub = u[b * GMLP_CHUNK:(b + 1) * GMLP_CHUNK, g * GROUP_CH:(g + 1) * GROUP_CH]
            mix_ref[b * GMLP_CHUNK:(b + 1) * GMLP_CHUNK,
                    g * GROUP_CH:(g + 1) * GROUP_CH] = (ub * sb).astype(BF16)


def _proj(x2d, seq, tm, chunk, nm, w_in_b, qg, kg, gvg, ws_t, bs_t):
    rows = x2d.shape[0]
    n_seq = rows // seq
    per_seq = seq // tm
    grid = (rows // tm,)
    row_spec = lambda w: pl.BlockSpec((tm, w), lambda i: (i, 0))
    out_shape = (
        jax.ShapeDtypeStruct((2, rows, 512), BF16),
        jax.ShapeDtypeStruct((n_seq, 512, seq), F32),
        jax.ShapeDtypeStruct((n_seq, 512, seq), BF16),
        jax.ShapeDtypeStruct((rows * N_HEADS, V_DIM), F32),
        jax.ShapeDtypeStruct((rows, 512), BF16),
        jax.ShapeDtypeStruct((rows, 512), BF16),
        jax.ShapeDtypeStruct((rows, 512), F32),
    )
    kt_spec = pl.BlockSpec((1, 512, tm), lambda i: (i // per_seq, 0, i % per_seq))
    return pl.pallas_call(
        functools.partial(_proj_kernel, chunk=chunk),
        grid=grid,
        in_specs=[
            row_spec(D_MODEL),
            _const_spec((1, D_MODEL)),
            _const_spec((D_MODEL, 2560)),
            _const_spec((1, 512)),
            _const_spec((1, 512)),
            _const_spec((1, 512)),
            _const_spec((N_GROUPS, GMLP_CHUNK, GMLP_CHUNK)),
            _const_spec((GMLP_CHUNK, N_GROUPS)),
        ],
        out_specs=(
            pl.BlockSpec((2, tm, 512), lambda i: (0, i, 0)),
            kt_spec, kt_spec,
            pl.BlockSpec((tm * N_HEADS, V_DIM), lambda i: (i, 0)),
            row_spec(512), row_spec(512), row_spec(512),
        ),
        out_shape=out_shape,
        compiler_params=pltpu.CompilerParams(
            dimension_semantics=("arbitrary",), vmem_limit_bytes=VMEM_LIMIT),
        name="proj",
    )(x2d, nm, w_in_b, qg, kg, gvg, ws_t, bs_t)


def _lambda(lq1_ref, lk1_ref, lq2_ref, lk2_ref, lam_init):
    a = jnp.sum(lq1_ref[...] * lk1_ref[...], axis=-1, keepdims=True)
    b = jnp.sum(lq2_ref[...] * lk2_ref[...], axis=-1, keepdims=True)
    return jnp.exp(a) - jnp.exp(b) + lam_init


def _sub_ln(att, gain, lam_init):
    r = lax.rsqrt(jnp.mean(att * att, axis=-1, keepdims=True) + EPS)
    return att * r * gain * (1.0 - lam_init)


def _attn_kernel(q_ref, kt_ref, v_ref, lq1_ref, lk1_ref, lq2_ref, lk2_ref, sg_ref,
                 o_ref, *, tq, nq, lam_init):
    ck = nq * tq

    def run_blocks(blocks):
        chunks = []
        for c in blocks:
            hi = (c + 1) * tq
            los = list(range(0, hi, ck))
            chunks += [(c, lo, min(lo + ck, hi), lo == los[-1]) for lo in los]
        q = {c: jnp.concatenate([q_ref[0, c * tq:(c + 1) * tq, :],
                                 q_ref[1, c * tq:(c + 1) * tq, :]], axis=0) for c in blocks}
        state = {c: None for c in blocks}

        def scores(c, lo, hi, last):
            s = jnp.dot(q[c], kt_ref[0, :, lo:hi], preferred_element_type=F32)
            if last:
                row = lax.broadcasted_iota(jnp.int32, s.shape, 0) % tq + c * tq
                col = lax.broadcasted_iota(jnp.int32, s.shape, 1) + lo
                s = jnp.where(col <= row, s, -jnp.inf)
            return s

        def fold(c, lo, hi, last, s):
            m_j = jnp.max(s, axis=-1, keepdims=True)
            if state[c] is None:
                m = m_j
                p = jnp.exp2(s - m)
                l = jnp.sum(p, axis=-1, keepdims=True)
                acc = jnp.dot(p.astype(BF16), v_ref[lo:hi, :], preferred_element_type=F32)
            else:
                m_old, l_old, acc_old = state[c]
                m = jnp.maximum(m_old, m_j)
                alpha = jnp.exp2(m_old - m)
                p = jnp.exp2(s - m)
                l = alpha * l_old + jnp.sum(p, axis=-1, keepdims=True)
                acc = alpha * acc_old + jnp.dot(p.astype(BF16), v_ref[lo:hi, :],
                                                preferred_element_type=F32)
            state[c] = (m, l, acc)
            if last:
                lam = _lambda(lq1_ref, lk1_ref, lq2_ref, lk2_ref, lam_init)
                o = acc / l
                att = o[:tq] - lam * o[tq:]
                o_ref[c * tq:(c + 1) * tq, :] = _sub_ln(att, sg_ref[...], lam_init).astype(o_ref.dtype)

        s_next = scores(*chunks[0])
        for i, ch in enumerate(chunks):
            s_cur = s_next
            if i + 1 < len(chunks):
                s_next = scores(*chunks[i + 1])
            fold(*ch, s_cur)

    run_blocks(tuple(range(nq)))


def _prompt_attn(q2, ktb, vb, lq1, lk1, lq2, lk2, sg, *, n_seq, seq, tq, lam_init):
    nq = seq // tq
    vec = lambda n: _const_spec((1, n))
    return pl.pallas_call(
        functools.partial(_attn_kernel, tq=tq, nq=nq, lam_init=lam_init),
        grid=(n_seq, N_HEADS),
        in_specs=[
            pl.BlockSpec((2, seq, HEAD_COLS), lambda b, h: (0, b, h)),
            pl.BlockSpec((1, HEAD_COLS, seq), lambda b, h: (b, h, 0)),
            pl.BlockSpec((seq, V_DIM), lambda b, h: (b, h)),
            vec(QK_DIM), vec(QK_DIM), vec(QK_DIM), vec(QK_DIM), vec(V_DIM),
        ],
        out_specs=pl.BlockSpec((seq, V_DIM), lambda b, h: (b, h)),
        out_shape=jax.ShapeDtypeStruct((n_seq * seq, ATT_WIDTH), BF16),
        compiler_params=pltpu.CompilerParams(
            dimension_semantics=("arbitrary", "arbitrary"),
            vmem_limit_bytes=VMEM_LIMIT),
        name="prompt_attn",
    )(q2, ktb, vb, lq1, lk1, lq2, lk2, sg)


Q_PAD = 8


GRP = 2 * Q_PAD


def _decode_init(b, q_ref, ktn_ref, vn_ref, qbd_sc, m_sc, l_sc, acc_sc, p_sc, alpha_sc, *, n_new):
    grp = GRP
    p_sc[...] = jnp.zeros(p_sc.shape, p_sc.dtype)
    alpha_sc[...] = jnp.ones(alpha_sc.shape, alpha_sc.dtype)
    lane = lax.broadcasted_iota(jnp.int32, (Q_PAD, 512), 1)
    for h in range(N_HEADS):
        own = (lane // HEAD_COLS) == h
        for c in range(2):
            r0 = h * grp + c * Q_PAD
            qbd_sc[r0:r0 + Q_PAD, :] = jnp.where(own, q_ref[c, 0], 0.0)
    s = jnp.dot(qbd_sc[...].astype(BF16), ktn_ref[0], preferred_element_type=F32)
    t = lax.broadcasted_iota(jnp.int32, s.shape, 0) % Q_PAD
    col = lax.broadcasted_iota(jnp.int32, s.shape, 1)
    ok = ((col // n_new) == b) & ((col % n_new) <= t)
    s = jnp.where(ok, s, -jnp.inf)
    m = jnp.max(s, axis=-1, keepdims=True)
    p = jnp.exp2(s - m)
    m_sc[...] = m
    l_sc[...] = jnp.sum(p, axis=-1, keepdims=True)
    pb = p.astype(BF16)
    for h in range(N_HEADS):
        acc_sc[h * grp:(h + 1) * grp, :] = jnp.dot(
            pb[h * grp:(h + 1) * grp], vn_ref[:, h * V_DIM:(h + 1) * V_DIM],
            preferred_element_type=F32)


def _decode_scores(k_refs, qbd_sc, m_sc, l_sc, p_sc, alpha_sc):
    kt = jnp.concatenate(
        [r[...].astype(BF16) for r in k_refs], axis=-1)
    s = jnp.dot(qbd_sc[...].astype(BF16), kt, preferred_element_type=F32)
    m_old = m_sc[...]
    m_new = jnp.maximum(m_old, jnp.max(s, axis=-1, keepdims=True))
    p = jnp.exp2(s - m_new)
    alpha = jnp.exp2(m_old - m_new)
    l_sc[...] = alpha * l_sc[...] + jnp.sum(p, axis=-1, keepdims=True)
    m_sc[...] = m_new
    p_sc[...] = p.astype(BF16)
    alpha_sc[...] = alpha


def _decode_values(v_refs, p_sc, alpha_sc, acc_sc):
    grp = GRP
    pb = p_sc[...]
    alpha = alpha_sc[...]
    for hp in range(N_HEADS // 2):
        vh = jnp.concatenate(
            [jnp.concatenate(
                [r[pl.ds(2 * hp + e, PAGE, stride=N_HEADS), :].astype(BF16) for e in range(2)],
                axis=-1) for r in v_refs], axis=0)
        pv = jnp.dot(pb[2 * hp * grp:(2 * hp + 2) * grp], vh, preferred_element_type=F32)
        for e in range(2):
            sl = slice((2 * hp + e) * grp, (2 * hp + e + 1) * grp)
            acc_sc[sl, :] = alpha[sl] * acc_sc[sl, :] + pv[e * grp:(e + 1) * grp,
                                                           e * V_DIM:(e + 1) * V_DIM]


def _decode_finish(lam_refs, sg_ref, o_ref, l_sc, acc_sc, *, lam_init):
    grp = GRP
    lam = _lambda(*lam_refs, lam_init)
    o = acc_sc[...] / l_sc[...]
    for h in range(N_HEADS):
        att = o[h * grp:h * grp + Q_PAD] - lam * o[h * grp + Q_PAD:(h + 1) * grp]
        o_ref[0, :, h * V_DIM:(h + 1) * V_DIM] = _sub_ln(
            att, sg_ref[...], lam_init).astype(o_ref.dtype)


def _mix_out(x_ref, att_ref, mix_ref, wo_ref, nf_ref, o_ref):
    o_ref[...] = (x_ref[...]
                  + jnp.dot(att_ref[...], wo_ref[:ATT_WIDTH, :], preferred_element_type=F32)
                  + jnp.dot(mix_ref[...], wo_ref[ATT_WIDTH:, :], preferred_element_type=F32))
    h = o_ref[...]
    ms = jnp.mean(h * h, axis=-1, keepdims=True)
    return (h * lax.rsqrt(ms + EPS) * nf_ref[...]).astype(BF16)


def _ffn_cols(hn, wu, wd):
    up = jnp.dot(hn, wu, preferred_element_type=F32)
    return jnp.dot(jnp.square(jnp.maximum(up, 0.0)).astype(BF16), wd, preferred_element_type=F32)


FF_BLK = 1024


def _post_kernel(x_ref, att_ref, mix_ref, wo_ref, nf_ref, wu_ref, wd_ref, o_ref):
    hn = _mix_out(x_ref, att_ref, mix_ref, wo_ref, nf_ref, o_ref)
    out = o_ref[...]
    for c in range(D_FF // FF_BLK):
        cols = slice(c * FF_BLK, (c + 1) * FF_BLK)
        out = out + _ffn_cols(hn, wu_ref[:, cols], wd_ref[cols, :])
    o_ref[...] = out


def _post(x2d, att, mix, wo_b, nf, wu_b, wd_b, tm):
    rows = x2d.shape[0]
    row_spec = lambda w: pl.BlockSpec((tm, w), lambda i: (i, 0))
    return pl.pallas_call(
        _post_kernel,
        grid=(rows // tm,),
        in_specs=[
            row_spec(D_MODEL), row_spec(ATT_WIDTH), row_spec(GMLP_WIDTH),
            _const_spec((D_MODEL, D_MODEL)),
            _const_spec((1, D_MODEL)),
            _const_spec((D_MODEL, D_FF)),
            _const_spec((D_FF, D_MODEL)),
        ],
        out_specs=row_spec(D_MODEL),
        out_shape=jax.ShapeDtypeStruct((rows, D_MODEL), F32),
        compiler_params=pltpu.CompilerParams(
            dimension_semantics=("arbitrary",), vmem_limit_bytes=VMEM_LIMIT),
        name="post",
    )(x2d, att, mix, wo_b, nf, wu_b, wd_b)


N_SLOTS = 4
AHEAD = N_SLOTS - 2
SUB = 16


def _post_decode_kernel(pt_ref, x_ref, att_ref, mix_ref, wo_ref, nf_ref, wu_ref, wd_ref,
                        q_ref, ktn_ref, vn_ref, lq1_ref, lk1_ref, lq2_ref, lk2_ref, sg_ref,
                        ck_hbm, cv_hbm, o_ref, os_ref,
                        hn_sc, a_sc, qbd_sc, m_sc, l_sc, acc_sc, p_sc, alpha_sc, kbuf, vbuf, sem,
                        *, n_pages, n_new, ff_blk, lam_init):
    i = pl.program_id(0)
    j = pl.program_id(1)
    n_steps = pl.num_programs(1)
    grp_per_seq = n_steps * SUB
    n_groups = pl.num_programs(0) * grp_per_seq
    g0 = (i * n_steps + j) * SUB

    def group_copies(g, wrap=False):
        s = g % N_SLOTS
        if wrap:
            g = g % n_groups
        seq, k = g // grp_per_seq, g % grp_per_seq
        cps = []
        for p in range(n_pages):
            page = pt_ref[seq, k * n_pages + p]
            cps.append(pltpu.make_async_copy(ck_hbm.at[page], kbuf.at[s, p], sem.at[s, 0]))
            cps.append(pltpu.make_async_copy(cv_hbm.at[page], vbuf.at[s, p], sem.at[s, 1]))
        return cps

    def k_pages(g):
        return [kbuf.at[g % N_SLOTS, p] for p in range(n_pages)]

    def v_pages(g):
        return [vbuf.at[g % N_SLOTS, p] for p in range(n_pages)]

    @pl.when(g0 == 0)
    def _():
        for g in range(AHEAD):
            for cp in group_copies(g):
                cp.start()
        vbuf[N_SLOTS - 1] = jnp.zeros(vbuf.shape[1:], vbuf.dtype)

    for cp in group_copies(g0 + AHEAD, wrap=True):
        cp.start()
    for cp in group_copies(g0):
        cp.wait()

    @pl.when(j == 0)
    def _():
        hn_sc[...] = _mix_out(x_ref, att_ref, mix_ref, wo_ref, nf_ref, o_ref)
        _decode_init(i, q_ref, ktn_ref, vn_ref, qbd_sc, m_sc, l_sc, acc_sc, p_sc, alpha_sc,
                     n_new=n_new)

    c0 = pl.multiple_of(j * ff_blk, ff_blk)
    for u in range(SUB):
        g = g0 + u
        if u:
            for cp in group_copies(g + AHEAD, wrap=True):
                cp.start()
            for cp in group_copies(g):
                cp.wait()
        _decode_values(v_pages(g + N_SLOTS - 1), p_sc, alpha_sc, acc_sc)
        _decode_scores(k_pages(g), qbd_sc, m_sc, l_sc, p_sc, alpha_sc)
        part = ff_blk // (SUB // 2)
        if u < SUB // 2:
            cu = pl.multiple_of(c0 + u * part, part)
            up = jnp.dot(hn_sc[...], wu_ref[:, pl.ds(cu, part)], preferred_element_type=F32)
            a_sc[:, u * part:(u + 1) * part] = jnp.square(jnp.maximum(up, 0.0)).astype(BF16)
        else:
            d = u - SUB // 2
            cd = pl.multiple_of(c0 + d * part, part)
            o_ref[...] += jnp.dot(a_sc[:, d * part:(d + 1) * part], wd_ref[pl.ds(cd, part), :],
                                  preferred_element_type=F32)

    @pl.when(j == n_steps - 1)
    def _():
        _decode_values(v_pages(g0 + SUB - 1), p_sc, alpha_sc, acc_sc)
        _decode_finish((lq1_ref, lk1_ref, lq2_ref, lk2_ref), sg_ref, os_ref, l_sc, acc_sc,
                       lam_init=lam_init)

    @pl.when(g0 + SUB == n_groups)
    def _():
        for g in range(AHEAD):
            for cp in group_copies(n_groups + g, wrap=True):
                cp.wait()


def _post_decode(page_table, x2d, att, mix, wo_b, nf, wu_b, wd_b,
                 qpad, ktn, vn, ckt, cv, lq1, lk1, lq2, lk2, sg, *, n_new, lam_init):
    rows = x2d.shape[0]
    n_dec, n_tab = page_table.shape
    tm = rows // n_dec
    steps = 16 // SUB
    ff_blk = D_FF // steps
    n_pages = n_tab // (steps * SUB)
    q_rows = N_HEADS * 2 * Q_PAD
    const = lambda shape: pl.BlockSpec(shape, lambda i, j, pt: (0,) * len(shape),
                                       pipeline_mode=pl.Buffered(1))
    row_spec = lambda w: pl.BlockSpec((tm, w), lambda i, j, pt: (i, 0))
    page_buf = pltpu.VMEM((N_SLOTS, n_pages, 512, PAGE), F32)
    grid_spec = pltpu.PrefetchScalarGridSpec(
        num_scalar_prefetch=1,
        grid=(n_dec, steps),
        in_specs=[
            row_spec(D_MODEL), row_spec(ATT_WIDTH), row_spec(GMLP_WIDTH),
            const((D_MODEL, D_MODEL)), const((1, D_MODEL)),
            const((D_MODEL, D_FF)), const((D_FF, D_MODEL)),
            pl.BlockSpec((2, 1, Q_PAD, 512), lambda i, j, pt: (0, i, 0, 0)),
            const((1, 512, n_dec * n_new)), const((n_dec * n_new, 512)),
            const((1, QK_DIM)), const((1, QK_DIM)), const((1, QK_DIM)), const((1, QK_DIM)),
            const((1, V_DIM)),
            pl.BlockSpec(memory_space=pl.ANY),
            pl.BlockSpec(memory_space=pl.ANY),
        ],
        out_specs=(row_spec(D_MODEL),
                   pl.BlockSpec((1, Q_PAD, ATT_WIDTH), lambda i, j, pt: (i, 0, 0))),
        scratch_shapes=[
            pltpu.VMEM((tm, D_MODEL), BF16),
            pltpu.VMEM((tm, ff_blk), BF16),
            pltpu.VMEM((q_rows, 512), F32),
            pltpu.VMEM((q_rows, 1), F32),
            pltpu.VMEM((q_rows, 1), F32),
            pltpu.VMEM((q_rows, V_DIM), F32),
            pltpu.VMEM((q_rows, n_pages * PAGE), BF16),
            pltpu.VMEM((q_rows, 1), F32),
            page_buf, page_buf,
            pltpu.SemaphoreType.DMA((N_SLOTS, 2)),
        ],
    )
    return pl.pallas_call(
        functools.partial(_post_decode_kernel, n_pages=n_pages, n_new=n_new, ff_blk=ff_blk,
                          lam_init=lam_init),
        grid_spec=grid_spec,
        out_shape=(jax.ShapeDtypeStruct((rows, D_MODEL), F32),
                   jax.ShapeDtypeStruct((n_dec, Q_PAD, ATT_WIDTH), F32)),
        compiler_params=pltpu.CompilerParams(
            dimension_semantics=("arbitrary", "arbitrary"), vmem_limit_bytes=VMEM_LIMIT),
        name="post_decode",
    )(page_table, x2d, att, mix, wo_b, nf, wu_b, wd_b, qpad, ktn, vn, lq1, lk1, lq2, lk2, sg,
      ckt, cv)


def kernel(x_prompt, x_sample, cache_k, cache_v, page_table, norm_mix, w_in, q_gain, k_gain,
           lambda_q1, lambda_k1, lambda_q2, lambda_k2, subln_gain, gv_gain, w_spatial,
           b_spatial, w_out, norm_ffn, w_up, w_down):
    depth = w_in.shape[0]
    assert depth == 1, "single-layer step"
    n_seq, seq, _ = x_prompt.shape
    n_dec, n_new, _ = x_sample.shape
    l = 0
    lam_init = 0.8 - 0.6 * math.exp(-0.3 * l)

    w_in_b = w_in[l].astype(BF16)
    wo_b = w_out[l].astype(BF16)
    wu_b = w_up[l].astype(BF16)
    wd_b = w_down[l].astype(BF16)
    qg = jnp.tile(q_gain[l], 2 * N_HEADS)[None]
    kg = jnp.tile(k_gain[l], 2 * N_HEADS)[None]
    gvg = gv_gain[l].reshape(1, GMLP_WIDTH)
    nm = norm_mix[l][None]
    nf = norm_ffn[l][None]
    sg = subln_gain[l][None]
    lq1, lk1, lq2, lk2 = (a[l][None] for a in (lambda_q1, lambda_k1, lambda_q2, lambda_k2))

    def gating_params(chunk):
        reps = GMLP_CHUNK // chunk
        bs_t = jnp.tile(b_spatial[l][:, :chunk], (1, reps)).T
        return w_spatial[l], bs_t

    xp = x_prompt.reshape(n_seq * seq, D_MODEL)
    q2, kt, ktb, v, vb, mix, _ = _proj(
        xp, seq, 512, GMLP_CHUNK, nm, w_in_b, qg, kg, gvg, *gating_params(GMLP_CHUNK))
    att = _prompt_attn(q2, ktb, vb, lq1, lk1, lq2, lk2, sg,
                       n_seq=n_seq, seq=seq, tq=256, lam_init=lam_init)
    k_prompt = kt.reshape(n_seq, N_HEADS, 2, QK_DIM, seq).transpose(0, 4, 1, 2, 3)[None]
    v_prompt = v.reshape(1, n_seq, seq, N_HEADS, V_DIM)

    rows_s = n_dec * n_new
    xs = x_sample.reshape(rows_s, D_MODEL)
    q2s, kts, ktbs, vs, vbs, mixs, gvs = _proj(
        xs, rows_s, rows_s, n_new, nm, w_in_b, qg, kg, gvg, *gating_params(n_new))
    qpad = jnp.pad(q2s.reshape(2, n_dec, n_new, 512).astype(F32),
                   ((0, 0), (0, 0), (0, Q_PAD - n_new), (0, 0)))
    n_phys = cache_k.shape[1]
    ckt = cache_k[l].transpose(0, 2, 3, 4, 1).reshape(n_phys, 512, PAGE)
    cv = cache_v[l].reshape(n_phys, PAGE * N_HEADS, V_DIM)
    y_prompt, att_s = _post_decode(page_table, xp, att, mix, wo_b, nf, wu_b, wd_b,
                                   qpad, ktbs, vbs, ckt, cv, lq1, lk1, lq2, lk2, sg,
                                   n_new=n_new, lam_init=lam_init)
    y_prompt = y_prompt.reshape(n_seq, seq, D_MODEL)
    att_s = att_s[:, :n_new].reshape(rows_s, ATT_WIDTH).astype(BF16)
    y_sample = _post(xs, att_s, mixs, wo_b, nf, wu_b, wd_b, rows_s).reshape(n_dec, n_new, D_MODEL)
    k_sample = kts[0].T.reshape(1, n_dec, n_new, N_HEADS, 2, QK_DIM)
    v_sample = vs.reshape(1, n_dec, n_new, N_HEADS, V_DIM)
    gv_sample = gvs.reshape(1, n_dec, n_new, GMLP_WIDTH)

    return (y_prompt, y_sample, k_prompt, v_prompt, k_sample, v_sample, gv_sample)
```

```python
import functools
import math

import jax
import jax.numpy as jnp
from jax import lax
from jax.experimental import pallas as pl
from jax.experimental.pallas import tpu as pltpu

F32 = jnp.float32
BF16 = jnp.bfloat16

D_MODEL = 1024
N_HEADS = 4
QK_DIM = 64
V_DIM = 128
HEAD_COLS = 2 * QK_DIM
ATT_WIDTH = N_HEADS * V_DIM
GMLP_WIDTH = 512
N_GROUPS = 4
GROUP_CH = 128
GMLP_CHUNK = 128
D_FF = 4096
PAGE = 128
EPS = 1e-6
LANES = 128
VMEM_LIMIT = 56 * 1024 * 1024

GELU_C = math.sqrt(2.0 / math.pi)
LOG2E = math.log2(math.e)


def _gelu(x):
    hx = 0.5 * x
    return hx + hx * jnp.tanh(x * (GELU_C + (GELU_C * 0.044715) * (x * x)))


def _const_spec(shape):
    nd = len(shape)
    return pl.BlockSpec(shape, lambda *_: (0,) * nd, pipeline_mode=pl.Buffered(1))


def _half_group_rms(t):
    lane = lax.broadcasted_iota(jnp.int32, (1, LANES), 1)
    lo = lane < QK_DIM
    outs = []
    for j in range(t.shape[1] // LANES):
        c = t[:, j * LANES:(j + 1) * LANES]
        c2 = c * c
        s_lo = jnp.sum(jnp.where(lo, c2, 0.0), axis=-1, keepdims=True)
        s_hi = jnp.sum(jnp.where(lo, 0.0, c2), axis=-1, keepdims=True)
        r = jnp.where(lo, lax.rsqrt(s_lo * (1.0 / QK_DIM) + EPS),
                      lax.rsqrt(s_hi * (1.0 / QK_DIM) + EPS))
        outs.append(c * r)
    return jnp.concatenate(outs, axis=-1)


def _proj_kernel(x_ref, nm_ref, w_ref, qg_ref, kg_ref, gvg_ref, ws_ref, bs_ref,
                 q_ref, kt_ref, ktb_ref, v_ref, vb_ref, mix_ref, gv_ref, *, chunk):
    tm = x_ref.shape[0]
    x = x_ref[...]
    ms = jnp.mean(x * x, axis=-1, keepdims=True)
    xn = (x * lax.rsqrt(ms + EPS) * nm_ref[...]).astype(BF16)

    def seg(i):
        return jnp.dot(xn, w_ref[:, i * 512:(i + 1) * 512], preferred_element_type=F32)

    lane = lax.broadcasted_iota(jnp.int32, (1, 512), 1)
    first_comp = (lane % HEAD_COLS) < QK_DIM

    z_u = seg(3)
    z_g = seg(4)
    u = _gelu(z_u)
    z_q = seg(0)
    gr = _gelu(z_g)
    z_k = seg(1)

    qn = _half_group_rms(z_q) * (qg_ref[...] * (QK_DIM ** -0.5 * LOG2E))
    q_ref[0] = jnp.where(first_comp, qn, 0.0).astype(BF16)
    q_ref[1] = jnp.where(first_comp, 0.0, qn).astype(BF16)
    v = seg(2)

    kn = _half_group_rms(z_k) * kg_ref[...]
    knt = kn.T
    kt_ref[0] = knt
    ktb_ref[0] = knt.astype(BF16)

    for h in range(N_HEADS):
        v_ref[pl.ds(h, tm, stride=N_HEADS), :] = v[:, h * V_DIM:(h + 1) * V_DIM]
    vb_ref[...] = v.astype(BF16)

    gvs = []
    for g in range(N_GROUPS):
        c = gr[:, g * GROUP_CH:(g + 1) * GROUP_CH]
        r = lax.rsqrt(jnp.mean(c * c, axis=-1, keepdims=True) + EPS)
        gvs.append(c * r * gvg_ref[:, g * GROUP_CH:(g + 1) * GROUP_CH])
    gv_ref[...] = jnp.concatenate(gvs, axis=-1)

    row = lax.broadcasted_iota(jnp.int32, (GMLP_CHUNK, GMLP_CHUNK), 0)
    col = lax.broadcasted_iota(jnp.int32, (GMLP_CHUNK, GMLP_CHUNK), 1)
    keep = (col <= row) & ((row // chunk) == (col // chunk))
    sel = jnp.where(col == row % chunk, 1.0, 0.0).astype(BF16)
    n_blk = tm // GMLP_CHUNK
    for g in range(N_GROUPS):
        wg = ws_ref[g].astype(BF16)
        if chunk < GMLP_CHUNK:
            wg = jnp.dot(sel, wg, preferred_element_type=F32).astype(BF16)
            wg = lax.dot_general(wg, sel, (((1,), (1,)), ((), ())),
                                 preferred_element_type=F32).astype(BF16)
        wm = jnp.where(keep, wg, jnp.zeros_like(wg))
        rhs = jnp.concatenate(
            [gvs[g][b * GMLP_CHUNK:(b + 1) * GMLP_CHUNK].astype(BF16) for b in range(n_blk)],
            axis=-1)
        s = jnp.dot(wm, rhs, preferred_element_type=F32)
        bias = bs_ref[:, g:g + 1]
        for b in range(n_blk):
            sb = s[:, b * GROUP_CH:(b + 1) * GROUP_CH] + bias
            ub = u[b * GMLP_CHUNK:(b + 1) * GMLP_CHUNK, g * GROUP_CH:(g + 1) * GROUP_CH]
            mix_ref[b * GMLP_CHUNK:(b + 1) * GMLP_CHUNK,
                    g * GROUP_CH:(g + 1) * GROUP_CH] = (ub * sb).astype(BF16)


def _proj(x2d, seq, tm, chunk, nm, w_in_b, qg, kg, gvg, ws_t, bs_t):
    rows = x2d.shape[0]
    n_seq = rows // seq
    per_seq = seq // tm
    grid = (rows // tm,)
    row_spec = lambda w: pl.BlockSpec((tm, w), lambda i: (i, 0))
    out_shape = (
        jax.ShapeDtypeStruct((2, rows, 512), BF16),
        jax.ShapeDtypeStruct((n_seq, 512, seq), F32),
        jax.ShapeDtypeStruct((n_seq, 512, seq), BF16),
        jax.ShapeDtypeStruct((rows * N_HEADS, V_DIM), F32),
        jax.ShapeDtypeStruct((rows, 512), BF16),
        jax.ShapeDtypeStruct((rows, 512), BF16),
        jax.ShapeDtypeStruct((rows, 512), F32),
    )
    kt_spec = pl.BlockSpec((1, 512, tm), lambda i: (i // per_seq, 0, i % per_seq))
    return pl.pallas_call(
        functools.partial(_proj_kernel, chunk=chunk),
        grid=grid,
        in_specs=[
            row_spec(D_MODEL),
            _const_spec((1, D_MODEL)),
            _const_spec((D_MODEL, 2560)),
            _const_spec((1, 512)),
            _const_spec((1, 512)),
            _const_spec((1, 512)),
            _const_spec((N_GROUPS, GMLP_CHUNK, GMLP_CHUNK)),
            _const_spec((GMLP_CHUNK, N_GROUPS)),
        ],
        out_specs=(
            pl.BlockSpec((2, tm, 512), lambda i: (0, i, 0)),
            kt_spec, kt_spec,
            pl.BlockSpec((tm * N_HEADS, V_DIM), lambda i: (i, 0)),
            row_spec(512), row_spec(512), row_spec(512),
        ),
        out_shape=out_shape,
        compiler_params=pltpu.CompilerParams(
            dimension_semantics=("arbitrary",), vmem_limit_bytes=VMEM_LIMIT),
        name="proj",
    )(x2d, nm, w_in_b, qg, kg, gvg, ws_t, bs_t)


def _lambda(lq1_ref, lk1_ref, lq2_ref, lk2_ref, lam_init):
    a = jnp.sum(lq1_ref[...] * lk1_ref[...], axis=-1, keepdims=True)
    b = jnp.sum(lq2_ref[...] * lk2_ref[...], axis=-1, keepdims=True)
    return jnp.exp(a) - jnp.exp(b) + lam_init


def _sub_ln(att, gain, lam_init):
    r = lax.rsqrt(jnp.mean(att * att, axis=-1, keepdims=True) + EPS)
    return att * r * gain * (1.0 - lam_init)


def _attn_kernel(q_ref, kt_ref, v_ref, lq1_ref, lk1_ref, lq2_ref, lk2_ref, sg_ref,
                 o_ref, *, tq, nq, lam_init):
    ck = nq * tq

    def run_blocks(blocks):
        chunks = []
        for c in blocks:
            hi = (c + 1) * tq
            los = list(range(0, hi, ck))
            chunks += [(c, lo, min(lo + ck, hi), lo == los[-1]) for lo in los]
        q = {c: jnp.concatenate([q_ref[0, c * tq:(c + 1) * tq, :],
                                 q_ref[1, c * tq:(c + 1) * tq, :]], axis=0) for c in blocks}
        state = {c: None for c in blocks}

        def scores(c, lo, hi, last):
            s = jnp.dot(q[c], kt_ref[0, :, lo:hi], preferred_element_type=F32)
            if last:
                row = lax.broadcasted_iota(jnp.int32, s.shape, 0) % tq + c * tq
                col = lax.broadcasted_iota(jnp.int32, s.shape, 1) + lo
                s = jnp.where(col <= row, s, -jnp.inf)
            return s

        def fold(c, lo, hi, last, s):
            m_j = jnp.max(s, axis=-1, keepdims=True)
            if state[c] is None:
                m = m_j
                p = jnp.exp2(s - m)
                l = jnp.sum(p, axis=-1, keepdims=True)
                acc = jnp.dot(p.astype(BF16), v_ref[lo:hi, :], preferred_element_type=F32)
            else:
                m_old, l_old, acc_old = state[c]
                m = jnp.maximum(m_old, m_j)
                alpha = jnp.exp2(m_old - m)
                p = jnp.exp2(s - m)
                l = alpha * l_old + jnp.sum(p, axis=-1, keepdims=True)
                acc = alpha * acc_old + jnp.dot(p.astype(BF16), v_ref[lo:hi, :],
                                                preferred_element_type=F32)
            state[c] = (m, l, acc)
            if last:
                lam = _lambda(lq1_ref, lk1_ref, lq2_ref, lk2_ref, lam_init)
                o = acc / l
                att = o[:tq] - lam * o[tq:]
                o_ref[c * tq:(c + 1) * tq, :] = _sub_ln(att, sg_ref[...], lam_init).astype(o_ref.dtype)

        s_next = scores(*chunks[0])
        for i, ch in enumerate(chunks):
            s_cur = s_next
            if i + 1 < len(chunks):
                s_next = scores(*chunks[i + 1])
            fold(*ch, s_cur)

    run_blocks(tuple(range(nq)))


def _prompt_attn(q2, ktb, vb, lq1, lk1, lq2, lk2, sg, *, n_seq, seq, tq, lam_init):
    nq = seq // tq
    vec = lambda n: _const_spec((1, n))
    return pl.pallas_call(
        functools.partial(_attn_kernel, tq=tq, nq=nq, lam_init=lam_init),
        grid=(n_seq, N_HEADS),
        in_specs=[
            pl.BlockSpec((2, seq, HEAD_COLS), lambda b, h: (0, b, h)),
            pl.BlockSpec((1, HEAD_COLS, seq), lambda b, h: (b, h, 0)),
            pl.BlockSpec((seq, V_DIM), lambda b, h: (b, h)),
            vec(QK_DIM), vec(QK_DIM), vec(QK_DIM), vec(QK_DIM), vec(V_DIM),
        ],
        out_specs=pl.BlockSpec((seq, V_DIM), lambda b, h: (b, h)),
        out_shape=jax.ShapeDtypeStruct((n_seq * seq, ATT_WIDTH), BF16),
        compiler_params=pltpu.CompilerParams(
            dimension_semantics=("arbitrary", "arbitrary"),
            vmem_limit_bytes=VMEM_LIMIT),
        name="prompt_attn",
    )(q2, ktb, vb, lq1, lk1, lq2, lk2, sg)


Q_PAD = 8


GRP = 2 * Q_PAD


def _decode_init(b, q_ref, ktn_ref, vn_ref, qbd_sc, m_sc, l_sc, acc_sc, p_sc, alpha_sc, *, n_new):
    grp = GRP
    p_sc[...] = jnp.zeros(p_sc.shape, p_sc.dtype)
    alpha_sc[...] = jnp.ones(alpha_sc.shape, alpha_sc.dtype)
    lane = lax.broadcasted_iota(jnp.int32, (Q_PAD, 512), 1)
    for h in range(N_HEADS):
        own = (lane // HEAD_COLS) == h
        for c in range(2):
            r0 = h * grp + c * Q_PAD
            qbd_sc[r0:r0 + Q_PAD, :] = jnp.where(own, q_ref[c, 0], 0.0)
    s = jnp.dot(qbd_sc[...].astype(BF16), ktn_ref[0], preferred_element_type=F32)
    t = lax.broadcasted_iota(jnp.int32, s.shape, 0) % Q_PAD
    col = lax.broadcasted_iota(jnp.int32, s.shape, 1)
    ok = ((col // n_new) == b) & ((col % n_new) <= t)
    s = jnp.where(ok, s, -jnp.inf)
    m = jnp.max(s, axis=-1, keepdims=True)
    p = jnp.exp2(s - m)
    m_sc[...] = m
    l_sc[...] = jnp.sum(p, axis=-1, keepdims=True)
    pb = p.astype(BF16)
    for h in range(N_HEADS):
        acc_sc[h * grp:(h + 1) * grp, :] = jnp.dot(
            pb[h * grp:(h + 1) * grp], vn_ref[:, h * V_DIM:(h + 1) * V_DIM],
            preferred_element_type=F32)


def _decode_scores(k_refs, qbd_sc, m_sc, l_sc, p_sc, alpha_sc):
    kt = jnp.concatenate(
        [r[...].astype(BF16) for r in k_refs], axis=-1)
    s = jnp.dot(qbd_sc[...].astype(BF16), kt, preferred_element_type=F32)
    m_old = m_sc[...]
    m_new = jnp.maximum(m_old, jnp.max(s, axis=-1, keepdims=True))
    p = jnp.exp2(s - m_new)
    alpha = jnp.exp2(m_old - m_new)
    l_sc[...] = alpha * l_sc[...] + jnp.sum(p, axis=-1, keepdims=True)
    m_sc[...] = m_new
    p_sc[...] = p.astype(BF16)
    alpha_sc[...] = alpha


def _decode_values(v_refs, p_sc, alpha_sc, acc_sc):
    grp = GRP
    pb = p_sc[...]
    alpha = alpha_sc[...]
    for hp in range(N_HEADS // 2):
        vh = jnp.concatenate(
            [jnp.concatenate(
                [r[pl.ds(2 * hp + e, PAGE, stride=N_HEADS), :].astype(BF16) for e in range(2)],
                axis=-1) for r in v_refs], axis=0)
        pv = jnp.dot(pb[2 * hp * grp:(2 * hp + 2) * grp], vh, preferred_element_type=F32)
        for e in range(2):
            sl = slice((2 * hp + e) * grp, (2 * hp + e + 1) * grp)
            acc_sc[sl, :] = alpha[sl] * acc_sc[sl, :] + pv[e * grp:(e + 1) * grp,
                                                           e * V_DIM:(e + 1) * V_DIM]


def _decode_finish(lam_refs, sg_ref, o_ref, l_sc, acc_sc, *, lam_init):
    grp = GRP
    lam = _lambda(*lam_refs, lam_init)
    o = acc_sc[...] / l_sc[...]
    for h in range(N_HEADS):
        att = o[h * grp:h * grp + Q_PAD] - lam * o[h * grp + Q_PAD:(h + 1) * grp]
        o_ref[0, :, h * V_DIM:(h + 1) * V_DIM] = _sub_ln(
            att, sg_ref[...], lam_init).astype(o_ref.dtype)


def _mix_out(x_ref, att_ref, mix_ref, wo_ref, nf_ref, o_ref):
    o_ref[...] = (x_ref[...]
                  + jnp.dot(att_ref[...], wo_ref[:ATT_WIDTH, :], preferred_element_type=F32)
                  + jnp.dot(mix_ref[...], wo_ref[ATT_WIDTH:, :], preferred_element_type=F32))
    h = o_ref[...]
    ms = jnp.mean(h * h, axis=-1, keepdims=True)
    return (h * lax.rsqrt(ms + EPS) * nf_ref[...]).astype(BF16)


def _ffn_cols(hn, wu, wd):
    up = jnp.dot(hn, wu, preferred_element_type=F32)
    return jnp.dot(jnp.square(jnp.maximum(up, 0.0)).astype(BF16), wd, preferred_element_type=F32)


FF_BLK = 1024


def _post_kernel(x_ref, att_ref, mix_ref, wo_ref, nf_ref, wu_ref, wd_ref, o_ref):
    hn = _mix_out(x_ref, att_ref, mix_ref, wo_ref, nf_ref, o_ref)
    out = o_ref[...]
    for c in range(D_FF // FF_BLK):
        cols = slice(c * FF_BLK, (c + 1) * FF_BLK)
        out = out + _ffn_cols(hn, wu_ref[:, cols], wd_ref[cols, :])
    o_ref[...] = out


def _post(x2d, att, mix, wo_b, nf, wu_b, wd_b, tm):
    rows = x2d.shape[0]
    row_spec = lambda w: pl.BlockSpec((tm, w), lambda i: (i, 0))
    return pl.pallas_call(
        _post_kernel,
        grid=(rows // tm,),
        in_specs=[
            row_spec(D_MODEL), row_spec(ATT_WIDTH), row_spec(GMLP_WIDTH),
            _const_spec((D_MODEL, D_MODEL)),
            _const_spec((1, D_MODEL)),
            _const_spec((D_MODEL, D_FF)),
            _const_spec((D_FF, D_MODEL)),
        ],
        out_specs=row_spec(D_MODEL),
        out_shape=jax.ShapeDtypeStruct((rows, D_MODEL), F32),
        compiler_params=pltpu.CompilerParams(
            dimension_semantics=("arbitrary",), vmem_limit_bytes=VMEM_LIMIT),
        name="post",
    )(x2d, att, mix, wo_b, nf, wu_b, wd_b)


N_SLOTS = 4
AHEAD = N_SLOTS - 2
SUB = 16


def _post_decode_kernel(pt_ref, x_ref, att_ref, mix_ref, wo_ref, nf_ref, wu_ref, wd_ref,
                        q_ref, ktn_ref, vn_ref, lq1_ref, lk1_ref, lq2_ref, lk2_ref, sg_ref,
                        ck_hbm, cv_hbm, o_ref, os_ref,
                        hn_sc, a_sc, qbd_sc, m_sc, l_sc, acc_sc, p_sc, alpha_sc, kbuf, vbuf, sem,
                        *, n_pages, n_new, ff_blk, lam_init):
    i = pl.program_id(0)
    j = pl.program_id(1)
    n_steps = pl.num_programs(1)
    grp_per_seq = n_steps * SUB
    n_groups = pl.num_programs(0) * grp_per_seq
    g0 = (i * n_steps + j) * SUB

    def group_copies(g, wrap=False):
        s = g % N_SLOTS
        if wrap:
            g = g % n_groups
        seq, k = g // grp_per_seq, g % grp_per_seq
        cps = []
        for p in range(n_pages):
            page = pt_ref[seq, k * n_pages + p]
            cps.append(pltpu.make_async_copy(ck_hbm.at[page], kbuf.at[s, p], sem.at[s, 0]))
            cps.append(pltpu.make_async_copy(cv_hbm.at[page], vbuf.at[s, p], sem.at[s, 1]))
        return cps

    def k_pages(g):
        return [kbuf.at[g % N_SLOTS, p] for p in range(n_pages)]

    def v_pages(g):
        return [vbuf.at[g % N_SLOTS, p] for p in range(n_pages)]

    @pl.when(g0 == 0)
    def _():
        for g in range(AHEAD):
            for cp in group_copies(g):
                cp.start()
        vbuf[N_SLOTS - 1] = jnp.zeros(vbuf.shape[1:], vbuf.dtype)

    for cp in group_copies(g0 + AHEAD, wrap=True):
        cp.start()
    for cp in group_copies(g0):
        cp.wait()

    @pl.when(j == 0)
    def _():
        hn_sc[...] = _mix_out(x_ref, att_ref, mix_ref, wo_ref, nf_ref, o_ref)
        _decode_init(i, q_ref, ktn_ref, vn_ref, qbd_sc, m_sc, l_sc, acc_sc, p_sc, alpha_sc,
                     n_new=n_new)

    c0 = pl.multiple_of(j * ff_blk, ff_blk)
    for u in range(SUB):
        g = g0 + u
        if u:
            for cp in group_copies(g + AHEAD, wrap=True):
                cp.start()
            for cp in group_copies(g):
                cp.wait()
        _decode_values(v_pages(g + N_SLOTS - 1), p_sc, alpha_sc, acc_sc)
        _decode_scores(k_pages(g), qbd_sc, m_sc, l_sc, p_sc, alpha_sc)
        part = ff_blk // (SUB // 2)
        if u < SUB // 2:
            cu = pl.multiple_of(c0 + u * part, part)
            up = jnp.dot(hn_sc[...], wu_ref[:, pl.ds(cu, part)], preferred_element_type=F32)
            a_sc[:, u * part:(u + 1) * part] = jnp.square(jnp.maximum(up, 0.0)).astype(BF16)
        else:
            d = u - SUB // 2
            cd = pl.multiple_of(c0 + d * part, part)
            o_ref[...] += jnp.dot(a_sc[:, d * part:(d + 1) * part], wd_ref[pl.ds(cd, part), :],
                                  preferred_element_type=F32)

    @pl.when(j == n_steps - 1)
    def _():
        _decode_values(v_pages(g0 + SUB - 1), p_sc, alpha_sc, acc_sc)
        _decode_finish((lq1_ref, lk1_ref, lq2_ref, lk2_ref), sg_ref, os_ref, l_sc, acc_sc,
                       lam_init=lam_init)

    @pl.when(g0 + SUB == n_groups)
    def _():
        for g in range(AHEAD):
            for cp in group_copies(n_groups + g, wrap=True):
                cp.wait()


def _post_decode(page_table, x2d, att, mix, wo_b, nf, wu_b, wd_b,
                 qpad, ktn, vn, ckt, cv, lq1, lk1, lq2, lk2, sg, *, n_new, lam_init):
    rows = x2d.shape[0]
    n_dec, n_tab = page_table.shape
    tm = rows // n_dec
    steps = 1
    ff_blk = D_FF // steps
    n_pages = n_tab // (steps * SUB)
    q_rows = N_HEADS * 2 * Q_PAD
    const = lambda shape: pl.BlockSpec(shape, lambda i, j, pt: (0,) * len(shape),
                                       pipeline_mode=pl.Buffered(1))
    row_spec = lambda w: pl.BlockSpec((tm, w), lambda i, j, pt: (i, 0))
    page_buf = pltpu.VMEM((N_SLOTS, n_pages, 512, PAGE), F32)
    grid_spec = pltpu.PrefetchScalarGridSpec(
        num_scalar_prefetch=1,
        grid=(n_dec, steps),
        in_specs=[
            row_spec(D_MODEL), row_spec(ATT_WIDTH), row_spec(GMLP_WIDTH),
            const((D_MODEL, D_MODEL)), const((1, D_MODEL)),
            const((D_MODEL, D_FF)), const((D_FF, D_MODEL)),
            pl.BlockSpec((2, 1, Q_PAD, 512), lambda i, j, pt: (0, i, 0, 0)),
            const((1, 512, n_dec * n_new)), const((n_dec * n_new, 512)),
            const((1, QK_DIM)), const((1, QK_DIM)), const((1, QK_DIM)), const((1, QK_DIM)),
            const((1, V_DIM)),
            pl.BlockSpec(memory_space=pl.ANY),
            pl.BlockSpec(memory_space=pl.ANY),
        ],
        out_specs=(row_spec(D_MODEL),
                   pl.BlockSpec((1, Q_PAD, ATT_WIDTH), lambda i, j, pt: (i, 0, 0))),
        scratch_shapes=[
            pltpu.VMEM((tm, D_MODEL), BF16),
            pltpu.VMEM((tm, ff_blk), BF16),
            pltpu.VMEM((q_rows, 512), F32),
            pltpu.VMEM((q_rows, 1), F32),
            pltpu.VMEM((q_rows, 1), F32),
            pltpu.VMEM((q_rows, V_DIM), F32),
            pltpu.VMEM((q_rows, n_pages * PAGE), BF16),
            pltpu.VMEM((q_rows, 1), F32),
            page_buf, page_buf,
            pltpu.SemaphoreType.DMA((N_SLOTS, 2)),
        ],
    )
    return pl.pallas_call(
        functools.partial(_post_decode_kernel, n_pages=n_pages, n_new=n_new, ff_blk=ff_blk,
                          lam_init=lam_init),
        grid_spec=grid_spec,
        out_shape=(jax.ShapeDtypeStruct((rows, D_MODEL), F32),
                   jax.ShapeDtypeStruct((n_dec, Q_PAD, ATT_WIDTH), F32)),
        compiler_params=pltpu.CompilerParams(
            dimension_semantics=("arbitrary", "arbitrary"), vmem_limit_bytes=VMEM_LIMIT),
        name="post_decode",
    )(page_table, x2d, att, mix, wo_b, nf, wu_b, wd_b, qpad, ktn, vn, lq1, lk1, lq2, lk2, sg,
      ckt, cv)


def kernel(x_prompt, x_sample, cache_k, cache_v, page_table, norm_mix, w_in, q_gain, k_gain,
           lambda_q1, lambda_k1, lambda_q2, lambda_k2, subln_gain, gv_gain, w_spatial,
           b_spatial, w_out, norm_ffn, w_up, w_down):
    depth = w_in.shape[0]
    assert depth == 1, "single-layer step"
    n_seq, seq, _ = x_prompt.shape
    n_dec, n_new, _ = x_sample.shape
    l = 0
    lam_init = 0.8 - 0.6 * math.exp(-0.3 * l)

    w_in_b = w_in[l].astype(BF16)
    wo_b = w_out[l].astype(BF16)
    wu_b = w_up[l].astype(BF16)
    wd_b = w_down[l].astype(BF16)
    qg = jnp.tile(q_gain[l], 2 * N_HEADS)[None]
    kg = jnp.tile(k_gain[l], 2 * N_HEADS)[None]
    gvg = gv_gain[l].reshape(1, GMLP_WIDTH)
    nm = norm_mix[l][None]
    nf = norm_ffn[l][None]
    sg = subln_gain[l][None]
    lq1, lk1, lq2, lk2 = (a[l][None] for a in (lambda_q1, lambda_k1, lambda_q2, lambda_k2))

    def gating_params(chunk):
        reps = GMLP_CHUNK // chunk
        bs_t = jnp.tile(b_spatial[l][:, :chunk], (1, reps)).T
        return w_spatial[l], bs_t

    xp = x_prompt.reshape(n_seq * seq, D_MODEL)
    q2, kt, ktb, v, vb, mix, _ = _proj(
        xp, seq, 512, GMLP_CHUNK, nm, w_in_b, qg, kg, gvg, *gating_params(GMLP_CHUNK))
    att = _prompt_attn(q2, ktb, vb, lq1, lk1, lq2, lk2, sg,
                       n_seq=n_seq, seq=seq, tq=256, lam_init=lam_init)
    k_prompt = kt.reshape(n_seq, N_HEADS, 2, QK_DIM, seq).transpose(0, 4, 1, 2, 3)[None]
    v_prompt = v.reshape(1, n_seq, seq, N_HEADS, V_DIM)

    rows_s = n_dec * n_new
    xs = x_sample.reshape(rows_s, D_MODEL)
    q2s, kts, ktbs, vs, vbs, mixs, gvs = _proj(
        xs, rows_s, rows_s, n_new, nm, w_in_b, qg, kg, gvg, *gating_params(n_new))
    qpad = jnp.pad(q2s.reshape(2, n_dec, n_new, 512).astype(F32),
                   ((0, 0), (0, 0), (0, Q_PAD - n_new), (0, 0)))
    n_phys = cache_k.shape[1]
    ckt = cache_k[l].transpose(0, 2, 3, 4, 1).reshape(n_phys, 512, PAGE)
    cv = cache_v[l].reshape(n_phys, PAGE * N_HEADS, V_DIM)
    y_prompt, att_s = _post_decode(page_table, xp, att, mix, wo_b, nf, wu_b, wd_b,
                                   qpad, ktbs, vbs, ckt, cv, lq1, lk1, lq2, lk2, sg,
                                   n_new=n_new, lam_init=lam_init)
    y_prompt = y_prompt.reshape(n_seq, seq, D_MODEL)
    att_s = att_s[:, :n_new].reshape(rows_s, ATT_WIDTH).astype(BF16)
    y_sample = _post(xs, att_s, mixs, wo_b, nf, wu_b, wd_b, rows_s).reshape(n_dec, n_new, D_MODEL)
    k_sample = kts[0].T.reshape(1, n_dec, n_new, N_HEADS, 2, QK_DIM)
    v_sample = vs.reshape(1, n_dec, n_new, N_HEADS, V_DIM)
    gv_sample = gvs.reshape(1, n_dec, n_new, GMLP_WIDTH)

    return (y_prompt, y_sample, k_prompt, v_prompt, k_sample, v_sample, gv_sample)
```

```python
import functools
import math

import jax
import jax.numpy as jnp
from jax import lax
from jax.experimental import pallas as pl
from jax.experimental.pallas import tpu as pltpu

F32 = jnp.float32
BF16 = jnp.bfloat16

D_MODEL = 1024
N_HEADS = 4
QK_DIM = 64
V_DIM = 128
HEAD_COLS = 2 * QK_DIM
ATT_WIDTH = N_HEADS * V_DIM
GMLP_WIDTH = 512
N_GROUPS = 4
GROUP_CH = 128
GMLP_CHUNK = 128
D_FF = 4096
PAGE = 128
EPS = 1e-6
LANES = 128
VMEM_LIMIT = 56 * 1024 * 1024

GELU_C = math.sqrt(2.0 / math.pi)
LOG2E = math.log2(math.e)


def _gelu(x):
    hx = 0.5 * x
    return hx + hx * jnp.tanh(x * (GELU_C + (GELU_C * 0.044715) * (x * x)))


def _const_spec(shape):
    nd = len(shape)
    return pl.BlockSpec(shape, lambda *_: (0,) * nd, pipeline_mode=pl.Buffered(1))


def _half_group_rms(t):
    lane = lax.broadcasted_iota(jnp.int32, (1, LANES), 1)
    lo = lane < QK_DIM
    outs = []
    for j in range(t.shape[1] // LANES):
        c = t[:, j * LANES:(j + 1) * LANES]
        c2 = c * c
        s_lo = jnp.sum(jnp.where(lo, c2, 0.0), axis=-1, keepdims=True)
        s_hi = jnp.sum(jnp.where(lo, 0.0, c2), axis=-1, keepdims=True)
        r = jnp.where(lo, lax.rsqrt(s_lo * (1.0 / QK_DIM) + EPS),
                      lax.rsqrt(s_hi * (1.0 / QK_DIM) + EPS))
        outs.append(c * r)
    return jnp.concatenate(outs, axis=-1)


def _proj_kernel(x_ref, nm_ref, w_ref, qg_ref, kg_ref, gvg_ref, ws_ref, bs_ref,
                 q_ref, kt_ref, ktb_ref, v_ref, vb_ref, mix_ref, gv_ref, *, chunk):
    tm = x_ref.shape[0]
    x = x_ref[...]
    ms = jnp.mean(x * x, axis=-1, keepdims=True)
    xn = (x * lax.rsqrt(ms + EPS) * nm_ref[...]).astype(BF16)

    def seg(i):
        return jnp.dot(xn, w_ref[:, i * 512:(i + 1) * 512], preferred_element_type=F32)

    lane = lax.broadcasted_iota(jnp.int32, (1, 512), 1)
    first_comp = (lane % HEAD_COLS) < QK_DIM

    z_u = seg(3)
    z_g = seg(4)
    u = _gelu(z_u)
    z_q = seg(0)
    gr = _gelu(z_g)
    z_k = seg(1)

    qn = _half_group_rms(z_q) * (qg_ref[...] * (QK_DIM ** -0.5 * LOG2E))
    q_ref[0] = jnp.where(first_comp, qn, 0.0).astype(BF16)
    q_ref[1] = jnp.where(first_comp, 0.0, qn).astype(BF16)
    v = seg(2)

    kn = _half_group_rms(z_k) * kg_ref[...]
    knt = kn.T
    kt_ref[0] = knt
    ktb_ref[0] = knt.astype(BF16)

    for h in range(N_HEADS):
        v_ref[pl.ds(h, tm, stride=N_HEADS), :] = v[:, h * V_DIM:(h + 1) * V_DIM]
    vb_ref[...] = v.astype(BF16)

    gvs = []
    for g in range(N_GROUPS):
        c = gr[:, g * GROUP_CH:(g + 1) * GROUP_CH]
        r = lax.rsqrt(jnp.mean(c * c, axis=-1, keepdims=True) + EPS)
        gvs.append(c * r * gvg_ref[:, g * GROUP_CH:(g + 1) * GROUP_CH])
    gv_ref[...] = jnp.concatenate(gvs, axis=-1)

    row = lax.broadcasted_iota(jnp.int32, (GMLP_CHUNK, GMLP_CHUNK), 0)
    col = lax.broadcasted_iota(jnp.int32, (GMLP_CHUNK, GMLP_CHUNK), 1)
    keep = (col <= row) & ((row // chunk) == (col // chunk))
    sel = jnp.where(col == row % chunk, 1.0, 0.0).astype(BF16)
    n_blk = tm // GMLP_CHUNK
    for g in range(N_GROUPS):
        wg = ws_ref[g].astype(BF16)
        if chunk < GMLP_CHUNK:
            wg = jnp.dot(sel, wg, preferred_element_type=F32).astype(BF16)
            wg = lax.dot_general(wg, sel, (((1,), (1,)), ((), ())),
                                 preferred_element_type=F32).astype(BF16)
        wm = jnp.where(keep, wg, jnp.zeros_like(wg))
        rhs = jnp.concatenate(
            [gvs[g][b * GMLP_CHUNK:(b + 1) * GMLP_CHUNK].astype(BF16) for b in range(n_blk)],
            axis=-1)
        s = jnp.dot(wm, rhs, preferred_element_type=F32)
        bias = bs_ref[:, g:g + 1]
        for b in range(n_blk):
            sb = s[:, b * GROUP_CH:(b + 1) * GROUP_CH] + bias
            ub = u[b * GMLP_CHUNK:(b + 1) * GMLP_CHUNK, g * GROUP_CH:(g + 1) * GROUP_CH]
            mix_ref[b * GMLP_CHUNK:(b + 1) * GMLP_CHUNK,
                    g * GROUP_CH:(g + 1) * GROUP_CH] = (ub * sb).astype(BF16)


def _proj(x2d, seq, tm, chunk, nm, w_in_b, qg, kg, gvg, ws_t, bs_t):
    rows = x2d.shape[0]
    n_seq = rows // seq
    per_seq = seq // tm
    grid = (rows // tm,)
    row_spec = lambda w: pl.BlockSpec((tm, w), lambda i: (i, 0))
    out_shape = (
        jax.ShapeDtypeStruct((2, rows, 512), BF16),
        jax.ShapeDtypeStruct((n_seq, 512, seq), F32),
        jax.ShapeDtypeStruct((n_seq, 512, seq), BF16),
        jax.ShapeDtypeStruct((rows * N_HEADS, V_DIM), F32),
        jax.ShapeDtypeStruct((rows, 512), BF16),
        jax.ShapeDtypeStruct((rows, 512), BF16),
        jax.ShapeDtypeStruct((rows, 512), F32),
    )
    kt_spec = pl.BlockSpec((1, 512, tm), lambda i: (i // per_seq, 0, i % per_seq))
    return pl.pallas_call(
        functools.partial(_proj_kernel, chunk=chunk),
        grid=grid,
        in_specs=[
            row_spec(D_MODEL),
            _const_spec((1, D_MODEL)),
            _const_spec((D_MODEL, 2560)),
            _const_spec((1, 512)),
            _const_spec((1, 512)),
            _const_spec((1, 512)),
            _const_spec((N_GROUPS, GMLP_CHUNK, GMLP_CHUNK)),
            _const_spec((GMLP_CHUNK, N_GROUPS)),
        ],
        out_specs=(
            pl.BlockSpec((2, tm, 512), lambda i: (0, i, 0)),
            kt_spec, kt_spec,
            pl.BlockSpec((tm * N_HEADS, V_DIM), lambda i: (i, 0)),
            row_spec(512), row_spec(512), row_spec(512),
        ),
        out_shape=out_shape,
        compiler_params=pltpu.CompilerParams(
            dimension_semantics=("arbitrary",), vmem_limit_bytes=VMEM_LIMIT),
        name="proj",
    )(x2d, nm, w_in_b, qg, kg, gvg, ws_t, bs_t)


def _lambda(lq1_ref, lk1_ref, lq2_ref, lk2_ref, lam_init):
    a = jnp.sum(lq1_ref[...] * lk1_ref[...], axis=-1, keepdims=True)
    b = jnp.sum(lq2_ref[...] * lk2_ref[...], axis=-1, keepdims=True)
    return jnp.exp(a) - jnp.exp(b) + lam_init


def _sub_ln(att, gain, lam_init):
    r = lax.rsqrt(jnp.mean(att * att, axis=-1, keepdims=True) + EPS)
    return att * r * gain * (1.0 - lam_init)


def _attn_kernel(q_ref, kt_ref, v_ref, lq1_ref, lk1_ref, lq2_ref, lk2_ref, sg_ref,
                 o_ref, *, tq, nq, lam_init):
    ck = nq * tq

    def run_blocks(blocks):
        chunks = []
        for c in blocks:
            hi = (c + 1) * tq
            los = list(range(0, hi, ck))
            chunks += [(c, lo, min(lo + ck, hi), lo == los[-1]) for lo in los]
        q = {c: jnp.concatenate([q_ref[0, c * tq:(c + 1) * tq, :],
                                 q_ref[1, c * tq:(c + 1) * tq, :]], axis=0) for c in blocks}
        state = {c: None for c in blocks}

        def scores(c, lo, hi, last):
            s = jnp.dot(q[c], kt_ref[0, :, lo:hi], preferred_element_type=F32)
            if last:
                row = lax.broadcasted_iota(jnp.int32, s.shape, 0) % tq + c * tq
                col = lax.broadcasted_iota(jnp.int32, s.shape, 1) + lo
                s = jnp.where(col <= row, s, -jnp.inf)
            return s

        def fold(c, lo, hi, last, s):
            m_j = jnp.max(s, axis=-1, keepdims=True)
            if state[c] is None:
                m = m_j
                p = jnp.exp2(s - m)
                l = jnp.sum(p, axis=-1, keepdims=True)
                acc = jnp.dot(p.astype(BF16), v_ref[lo:hi, :], preferred_element_type=F32)
            else:
                m_old, l_old, acc_old = state[c]
                m = jnp.maximum(m_old, m_j)
                alpha = jnp.exp2(m_old - m)
                p = jnp.exp2(s - m)
                l = alpha * l_old + jnp.sum(p, axis=-1, keepdims=True)
                acc = alpha * acc_old + jnp.dot(p.astype(BF16), v_ref[lo:hi, :],
                                                preferred_element_type=F32)
            state[c] = (m, l, acc)
            if last:
                lam = _lambda(lq1_ref, lk1_ref, lq2_ref, lk2_ref, lam_init)
                o = acc / l
                att = o[:tq] - lam * o[tq:]
                o_ref[c * tq:(c + 1) * tq, :] = _sub_ln(att, sg_ref[...], lam_init).astype(o_ref.dtype)

        s_next = scores(*chunks[0])
        for i, ch in enumerate(chunks):
            s_cur = s_next
            if i + 1 < len(chunks):
                s_next = scores(*chunks[i + 1])
            fold(*ch, s_cur)

    run_blocks(tuple(range(nq)))


def _prompt_attn(q2, ktb, vb, lq1, lk1, lq2, lk2, sg, *, n_seq, seq, tq, lam_init):
    nq = seq // tq
    vec = lambda n: _const_spec((1, n))
    return pl.pallas_call(
        functools.partial(_attn_kernel, tq=tq, nq=nq, lam_init=lam_init),
        grid=(n_seq, N_HEADS),
        in_specs=[
            pl.BlockSpec((2, seq, HEAD_COLS), lambda b, h: (0, b, h)),
            pl.BlockSpec((1, HEAD_COLS, seq), lambda b, h: (b, h, 0)),
            pl.BlockSpec((seq, V_DIM), lambda b, h: (b, h)),
            vec(QK_DIM), vec(QK_DIM), vec(QK_DIM), vec(QK_DIM), vec(V_DIM),
        ],
        out_specs=pl.BlockSpec((seq, V_DIM), lambda b, h: (b, h)),
        out_shape=jax.ShapeDtypeStruct((n_seq * seq, ATT_WIDTH), BF16),
        compiler_params=pltpu.CompilerParams(
            dimension_semantics=("arbitrary", "arbitrary"),
            vmem_limit_bytes=VMEM_LIMIT),
        name="prompt_attn",
    )(q2, ktb, vb, lq1, lk1, lq2, lk2, sg)


Q_PAD = 8


GRP = 2 * Q_PAD


def _decode_init(b, q_ref, ktn_ref, vn_ref, qbd_sc, m_sc, l_sc, acc_sc, p_sc, alpha_sc, *, n_new):
    grp = GRP
    p_sc[...] = jnp.zeros(p_sc.shape, p_sc.dtype)
    alpha_sc[...] = jnp.ones(alpha_sc.shape, alpha_sc.dtype)
    lane = lax.broadcasted_iota(jnp.int32, (Q_PAD, 512), 1)
    for h in range(N_HEADS):
        own = (lane // HEAD_COLS) == h
        for c in range(2):
            r0 = h * grp + c * Q_PAD
            qbd_sc[r0:r0 + Q_PAD, :] = jnp.where(own, q_ref[c, 0], 0.0)
    s = jnp.dot(qbd_sc[...].astype(BF16), ktn_ref[0], preferred_element_type=F32)
    t = lax.broadcasted_iota(jnp.int32, s.shape, 0) % Q_PAD
    col = lax.broadcasted_iota(jnp.int32, s.shape, 1)
    ok = ((col // n_new) == b) & ((col % n_new) <= t)
    s = jnp.where(ok, s, -jnp.inf)
    m = jnp.max(s, axis=-1, keepdims=True)
    p = jnp.exp2(s - m)
    m_sc[...] = m
    l_sc[...] = jnp.sum(p, axis=-1, keepdims=True)
    pb = p.astype(BF16)
    for h in range(N_HEADS):
        acc_sc[h * grp:(h + 1) * grp, :] = jnp.dot(
            pb[h * grp:(h + 1) * grp], vn_ref[:, h * V_DIM:(h + 1) * V_DIM],
            preferred_element_type=F32)


def _decode_scores(k_refs, qbd_sc, m_sc, l_sc, p_sc, alpha_sc):
    kt = jnp.concatenate(
        [r[...].astype(BF16) for r in k_refs], axis=-1)
    s = jnp.dot(qbd_sc[...].astype(BF16), kt, preferred_element_type=F32)
    m_old = m_sc[...]
    m_new = jnp.maximum(m_old, jnp.max(s, axis=-1, keepdims=True))
    p = jnp.exp2(s - m_new)
    alpha = jnp.exp2(m_old - m_new)
    l_sc[...] = alpha * l_sc[...] + jnp.sum(p, axis=-1, keepdims=True)
    m_sc[...] = m_new
    p_sc[...] = p.astype(BF16)
    alpha_sc[...] = alpha


def _decode_values(v_refs, p_sc, alpha_sc, acc_sc):
    grp = GRP
    pb = p_sc[...]
    alpha = alpha_sc[...]
    for hp in range(N_HEADS // 2):
        vh = jnp.concatenate(
            [jnp.concatenate(
                [r[pl.ds(2 * hp + e, PAGE, stride=N_HEADS), :].astype(BF16) for e in range(2)],
                axis=-1) for r in v_refs], axis=0)
        pv = jnp.dot(pb[2 * hp * grp:(2 * hp + 2) * grp], vh, preferred_element_type=F32)
        for e in range(2):
            sl = slice((2 * hp + e) * grp, (2 * hp + e + 1) * grp)
            acc_sc[sl, :] = alpha[sl] * acc_sc[sl, :] + pv[e * grp:(e + 1) * grp,
                                                           e * V_DIM:(e + 1) * V_DIM]


def _decode_finish(lam_refs, sg_ref, o_ref, l_sc, acc_sc, *, lam_init):
    grp = GRP
    lam = _lambda(*lam_refs, lam_init)
    o = acc_sc[...] / l_sc[...]
    for h in range(N_HEADS):
        att = o[h * grp:h * grp + Q_PAD] - lam * o[h * grp + Q_PAD:(h + 1) * grp]
        o_ref[0, :, h * V_DIM:(h + 1) * V_DIM] = _sub_ln(
            att, sg_ref[...], lam_init).astype(o_ref.dtype)


def _mix_out(x_ref, att_ref, mix_ref, wo_ref, nf_ref, o_ref):
    o_ref[...] = (x_ref[...]
                  + jnp.dot(att_ref[...], wo_ref[:ATT_WIDTH, :], preferred_element_type=F32)
                  + jnp.dot(mix_ref[...], wo_ref[ATT_WIDTH:, :], preferred_element_type=F32))
    h = o_ref[...]
    ms = jnp.mean(h * h, axis=-1, keepdims=True)
    return (h * lax.rsqrt(ms + EPS) * nf_ref[...]).astype(BF16)


def _ffn_cols(hn, wu, wd):
    up = jnp.dot(hn, wu, preferred_element_type=F32)
    return jnp.dot(jnp.square(jnp.maximum(up, 0.0)).astype(BF16), wd, preferred_element_type=F32)


FF_BLK = 1024


def _post_kernel(x_ref, att_ref, mix_ref, wo_ref, nf_ref, wu_ref, wd_ref, o_ref):
    hn = _mix_out(x_ref, att_ref, mix_ref, wo_ref, nf_ref, o_ref)
    out = o_ref[...]
    for c in range(D_FF // FF_BLK):
        cols = slice(c * FF_BLK, (c + 1) * FF_BLK)
        out = out + _ffn_cols(hn, wu_ref[:, cols], wd_ref[cols, :])
    o_ref[...] = out


def _post(x2d, att, mix, wo_b, nf, wu_b, wd_b, tm):
    rows = x2d.shape[0]
    row_spec = lambda w: pl.BlockSpec((tm, w), lambda i: (i, 0))
    return pl.pallas_call(
        _post_kernel,
        grid=(rows // tm,),
        in_specs=[
            row_spec(D_MODEL), row_spec(ATT_WIDTH), row_spec(GMLP_WIDTH),
            _const_spec((D_MODEL, D_MODEL)),
            _const_spec((1, D_MODEL)),
            _const_spec((D_MODEL, D_FF)),
            _const_spec((D_FF, D_MODEL)),
        ],
        out_specs=row_spec(D_MODEL),
        out_shape=jax.ShapeDtypeStruct((rows, D_MODEL), F32),
        compiler_params=pltpu.CompilerParams(
            dimension_semantics=("arbitrary",), vmem_limit_bytes=VMEM_LIMIT),
        name="post",
    )(x2d, att, mix, wo_b, nf, wu_b, wd_b)


N_SLOTS = 4
AHEAD = N_SLOTS - 2
SUB = 16


def _post_decode_kernel(pt_ref, x_ref, att_ref, mix_ref, wo_ref, nf_ref, wu_ref, wd_ref,
                        q_ref, ktn_ref, vn_ref, lq1_ref, lk1_ref, lq2_ref, lk2_ref, sg_ref,
                        ck_hbm, cv_hbm, o_ref, os_ref,
                        hn_sc, a_sc, qbd_sc, m_sc, l_sc, acc_sc, p_sc, alpha_sc, kbuf, vbuf, sem,
                        *, n_pages, n_new, ff_blk, lam_init):
    i = pl.program_id(0)
    j = pl.program_id(1)
    n_steps = pl.num_programs(1)
    grp_per_seq = n_steps * SUB
    n_groups = pl.num_programs(0) * grp_per_seq
    g0 = (i * n_steps + j) * SUB

    def group_copies(g, wrap=False):
        s = g % N_SLOTS
        if wrap:
            g = g % n_groups
        seq, k = g // grp_per_seq, g % grp_per_seq
        cps = []
        for p in range(n_pages):
            page = pt_ref[seq, k * n_pages + p]
            cps.append(pltpu.make_async_copy(ck_hbm.at[page], kbuf.at[s, p], sem.at[s, 0]))
            cps.append(pltpu.make_async_copy(cv_hbm.at[page], vbuf.at[s, p], sem.at[s, 1]))
        return cps

    def start_group(g, wrap=False):
        for n, cp in enumerate(group_copies(g, wrap)):
            cp.start(priority=n % 2)

    def k_pages(g):
        return [kbuf.at[g % N_SLOTS, p] for p in range(n_pages)]

    def v_pages(g):
        return [vbuf.at[g % N_SLOTS, p] for p in range(n_pages)]

    @pl.when(g0 == 0)
    def _():
        for g in range(AHEAD):
            start_group(g)
        vbuf[N_SLOTS - 1] = jnp.zeros(vbuf.shape[1:], vbuf.dtype)

    start_group(g0 + AHEAD, wrap=True)
    for cp in group_copies(g0):
        cp.wait()

    @pl.when(j == 0)
    def _():
        hn_sc[...] = _mix_out(x_ref, att_ref, mix_ref, wo_ref, nf_ref, o_ref)
        _decode_init(i, q_ref, ktn_ref, vn_ref, qbd_sc, m_sc, l_sc, acc_sc, p_sc, alpha_sc,
                     n_new=n_new)

    c0 = pl.multiple_of(j * ff_blk, ff_blk)
    for u in range(SUB):
        g = g0 + u
        if u:
            start_group(g + AHEAD, wrap=True)
            for cp in group_copies(g):
                cp.wait()
        _decode_values(v_pages(g + N_SLOTS - 1), p_sc, alpha_sc, acc_sc)
        _decode_scores(k_pages(g), qbd_sc, m_sc, l_sc, p_sc, alpha_sc)
        part = ff_blk // (SUB // 2)
        if u < SUB // 2:
            cu = pl.multiple_of(c0 + u * part, part)
            up = jnp.dot(hn_sc[...], wu_ref[:, pl.ds(cu, part)], preferred_element_type=F32)
            a_sc[:, u * part:(u + 1) * part] = jnp.square(jnp.maximum(up, 0.0)).astype(BF16)
        else:
            d = u - SUB // 2
            cd = pl.multiple_of(c0 + d * part, part)
            o_ref[...] += jnp.dot(a_sc[:, d * part:(d + 1) * part], wd_ref[pl.ds(cd, part), :],
                                  preferred_element_type=F32)

    @pl.when(j == n_steps - 1)
    def _():
        _decode_values(v_pages(g0 + SUB - 1), p_sc, alpha_sc, acc_sc)
        _decode_finish((lq1_ref, lk1_ref, lq2_ref, lk2_ref), sg_ref, os_ref, l_sc, acc_sc,
                       lam_init=lam_init)

    @pl.when(g0 + SUB == n_groups)
    def _():
        for g in range(AHEAD):
            for cp in group_copies(n_groups + g, wrap=True):
                cp.wait()


def _post_decode(page_table, x2d, att, mix, wo_b, nf, wu_b, wd_b,
                 qpad, ktn, vn, ckt, cv, lq1, lk1, lq2, lk2, sg, *, n_new, lam_init):
    rows = x2d.shape[0]
    n_dec, n_tab = page_table.shape
    tm = rows // n_dec
    steps = 1
    ff_blk = D_FF // steps
    n_pages = n_tab // (steps * SUB)
    q_rows = N_HEADS * 2 * Q_PAD
    const = lambda shape: pl.BlockSpec(shape, lambda i, j, pt: (0,) * len(shape),
                                       pipeline_mode=pl.Buffered(1))
    row_spec = lambda w: pl.BlockSpec((tm, w), lambda i, j, pt: (i, 0))
    page_buf = pltpu.VMEM((N_SLOTS, n_pages, 512, PAGE), F32)
    grid_spec = pltpu.PrefetchScalarGridSpec(
        num_scalar_prefetch=1,
        grid=(n_dec, steps),
        in_specs=[
            row_spec(D_MODEL), row_spec(ATT_WIDTH), row_spec(GMLP_WIDTH),
            const((D_MODEL, D_MODEL)), const((1, D_MODEL)),
            const((D_MODEL, D_FF)), const((D_FF, D_MODEL)),
            pl.BlockSpec((2, 1, Q_PAD, 512), lambda i, j, pt: (0, i, 0, 0)),
            const((1, 512, n_dec * n_new)), const((n_dec * n_new, 512)),
            const((1, QK_DIM)), const((1, QK_DIM)), const((1, QK_DIM)), const((1, QK_DIM)),
            const((1, V_DIM)),
            pl.BlockSpec(memory_space=pl.ANY),
            pl.BlockSpec(memory_space=pl.ANY),
        ],
        out_specs=(row_spec(D_MODEL),
                   pl.BlockSpec((1, Q_PAD, ATT_WIDTH), lambda i, j, pt: (i, 0, 0))),
        scratch_shapes=[
            pltpu.VMEM((tm, D_MODEL), BF16),
            pltpu.VMEM((tm, ff_blk), BF16),
            pltpu.VMEM((q_rows, 512), F32),
            pltpu.VMEM((q_rows, 1), F32),
            pltpu.VMEM((q_rows, 1), F32),
            pltpu.VMEM((q_rows, V_DIM), F32),
            pltpu.VMEM((q_rows, n_pages * PAGE), BF16),
            pltpu.VMEM((q_rows, 1), F32),
            page_buf, page_buf,
            pltpu.SemaphoreType.DMA((N_SLOTS, 2)),
        ],
    )
    return pl.pallas_call(
        functools.partial(_post_decode_kernel, n_pages=n_pages, n_new=n_new, ff_blk=ff_blk,
                          lam_init=lam_init),
        grid_spec=grid_spec,
        out_shape=(jax.ShapeDtypeStruct((rows, D_MODEL), F32),
                   jax.ShapeDtypeStruct((n_dec, Q_PAD, ATT_WIDTH), F32)),
        compiler_params=pltpu.CompilerParams(
            dimension_semantics=("arbitrary", "arbitrary"), vmem_limit_bytes=VMEM_LIMIT),
        name="post_decode",
    )(page_table, x2d, att, mix, wo_b, nf, wu_b, wd_b, qpad, ktn, vn, lq1, lk1, lq2, lk2, sg,
      ckt, cv)


def kernel(x_prompt, x_sample, cache_k, cache_v, page_table, norm_mix, w_in, q_gain, k_gain,
           lambda_q1, lambda_k1, lambda_q2, lambda_k2, subln_gain, gv_gain, w_spatial,
           b_spatial, w_out, norm_ffn, w_up, w_down):
    depth = w_in.shape[0]
    assert depth == 1, "single-layer step"
    n_seq, seq, _ = x_prompt.shape
    n_dec, n_new, _ = x_sample.shape
    l = 0
    lam_init = 0.8 - 0.6 * math.exp(-0.3 * l)

    w_in_b = w_in[l].astype(BF16)
    wo_b = w_out[l].astype(BF16)
    wu_b = w_up[l].astype(BF16)
    wd_b = w_down[l].astype(BF16)
    qg = jnp.tile(q_gain[l], 2 * N_HEADS)[None]
    kg = jnp.tile(k_gain[l], 2 * N_HEADS)[None]
    gvg = gv_gain[l].reshape(1, GMLP_WIDTH)
    nm = norm_mix[l][None]
    nf = norm_ffn[l][None]
    sg = subln_gain[l][None]
    lq1, lk1, lq2, lk2 = (a[l][None] for a in (lambda_q1, lambda_k1, lambda_q2, lambda_k2))

    def gating_params(chunk):
        reps = GMLP_CHUNK // chunk
        bs_t = jnp.tile(b_spatial[l][:, :chunk], (1, reps)).T
        return w_spatial[l], bs_t

    xp = x_prompt.reshape(n_seq * seq, D_MODEL)
    q2, kt, ktb, v, vb, mix, _ = _proj(
        xp, seq, 512, GMLP_CHUNK, nm, w_in_b, qg, kg, gvg, *gating_params(GMLP_CHUNK))
    att = _prompt_attn(q2, ktb, vb, lq1, lk1, lq2, lk2, sg,
                       n_seq=n_seq, seq=seq, tq=256, lam_init=lam_init)
    k_prompt = kt.reshape(n_seq, N_HEADS, 2, QK_DIM, seq).transpose(0, 4, 1, 2, 3)[None]
    v_prompt = v.reshape(1, n_seq, seq, N_HEADS, V_DIM)

    rows_s = n_dec * n_new
    xs = x_sample.reshape(rows_s, D_MODEL)
    q2s, kts, ktbs, vs, vbs, mixs, gvs = _proj(
        xs, rows_s, rows_s, n_new, nm, w_in_b, qg, kg, gvg, *gating_params(n_new))
    qpad = jnp.pad(q2s.reshape(2, n_dec, n_new, 512).astype(F32),
                   ((0, 0), (0, 0), (0, Q_PAD - n_new), (0, 0)))
    n_phys = cache_k.shape[1]
    ckt = cache_k[l].transpose(0, 2, 3, 4, 1).reshape(n_phys, 512, PAGE)
    cv = cache_v[l].reshape(n_phys, PAGE * N_HEADS, V_DIM)
    y_prompt, att_s = _post_decode(page_table, xp, att, mix, wo_b, nf, wu_b, wd_b,
                                   qpad, ktbs, vbs, ckt, cv, lq1, lk1, lq2, lk2, sg,
                                   n_new=n_new, lam_init=lam_init)
    y_prompt = y_prompt.reshape(n_seq, seq, D_MODEL)
    att_s = att_s[:, :n_new].reshape(rows_s, ATT_WIDTH).astype(BF16)
    y_sample = _post(xs, att_s, mixs, wo_b, nf, wu_b, wd_b, rows_s).reshape(n_dec, n_new, D_MODEL)
    k_sample = kts[0].T.reshape(1, n_dec, n_new, N_HEADS, 2, QK_DIM)
    v_sample = vs.reshape(1, n_dec, n_new, N_HEADS, V_DIM)
    gv_sample = gvs.reshape(1, n_dec, n_new, GMLP_WIDTH)

    return (y_prompt, y_sample, k_prompt, v_prompt, k_sample, v_sample, gv_sample)
```
